```python
import math
import jax, jax.numpy as jnp
from jax import lax
import numpy as np

D_MODEL = 1024
BATCH = 32
SEQ = 256
DEPTH = 4
DEC_BATCH = 2
DEC_SEQ = 4096
PAST_LEN = 512

GRID_W = 64
N_MIXERS = 2
N_MLA = (DEPTH + 1) // 2
N_FOURIER = DEPTH // 2
N_HEADS = 8
QK_NOPE = 128
QK_ROPE = 64
V_DIM = 128
Q_LORA = 512
KV_LORA = 256
AXIS_FREQS = QK_ROPE // 4
ROPE_THETA = 10000.0
N_FGROUPS = 4
FGROUP_DIM = D_MODEL // N_FGROUPS
D_FF = 2816
N_MOD = 9
Q_BLOCK = 128
EPS = 1e-6
ATTN_SCALE = 1.0 / math.sqrt(QK_NOPE + QK_ROPE)

kernel_name = "mla_fnet_macaron_dit_step"


def _rms(x, g):
    xf = x.astype(jnp.float32)
    y = xf * lax.rsqrt(jnp.mean(xf * xf, axis=-1, keepdims=True) + EPS)
    return (y * g.astype(jnp.float32)).astype(x.dtype)


def _modulate(h, shift, scale):
    return h * (1 + scale) + shift


def _modulation(cond, w, b):
    m = jax.nn.silu(cond) @ w + b
    return jnp.split(m, N_MOD, axis=-1)


def _swiglu(h, wg, wu, wd):
    return (jax.nn.silu(h @ wg) * (h @ wu)) @ wd


def _axial_rope_tables(n):
    rows = n // GRID_W
    row = jnp.repeat(jnp.arange(rows, dtype=jnp.float32), GRID_W)
    col = jnp.tile(jnp.arange(GRID_W, dtype=jnp.float32), rows)
    inv = 1.0 / (ROPE_THETA ** (jnp.arange(AXIS_FREQS, dtype=jnp.float32) / AXIS_FREQS))
    ang = jnp.stack([row[:, None] * inv, col[:, None] * inv], axis=1)
    return jnp.cos(ang), jnp.sin(ang)


def _apply_axial_rope(x, cos, sin):
    xs = x.astype(jnp.float32).reshape(x.shape[:-1] + (2, 2, AXIS_FREQS))
    x1 = xs[..., 0, :]
    x2 = xs[..., 1, :]
    c = cos[:, None]
    s = sin[:, None]
    out = jnp.stack([x1 * c - x2 * s, x2 * c + x1 * s], axis=-2)
    return out.reshape(x.shape).astype(x.dtype)


def _mla_qkv(h, w_dq, q_norm, w_uq, w_dkv, kv_norm, rope):
    B, S, _ = h.shape
    cq = _rms(h @ w_dq, q_norm)
    q = (cq @ w_uq).reshape(B, S, N_HEADS, QK_NOPE + QK_ROPE)
    kv = h @ w_dkv
    ckv = _rms(kv[..., :KV_LORA], kv_norm)
    kpe = kv[..., KV_LORA:]
    if rope is not None:
        cos, sin = rope
        q = jnp.concatenate([q[..., :QK_NOPE], _apply_axial_rope(q[..., QK_NOPE:], cos, sin)], axis=-1)
        kpe = _apply_axial_rope(kpe[:, :, None, :], cos, sin)[:, :, 0, :]
    return q, ckv, kpe


def _mla_up(ckv, kpe, w_ukv):
    B, T, _ = ckv.shape
    kvu = (ckv @ w_ukv).reshape(B, T, N_HEADS, QK_NOPE + V_DIM)
    k = jnp.concatenate(
        [kvu[..., :QK_NOPE], jnp.broadcast_to(kpe[:, :, None, :], (B, T, N_HEADS, QK_ROPE))], axis=-1)
    return k, kvu[..., QK_NOPE:]


def _attend(q, k, v):
    B, S, H, Dk = q.shape
    qb = Q_BLOCK if S % Q_BLOCK == 0 else S
    nb = S // qb
    kf = k.astype(jnp.float32)
    vf = v.astype(jnp.float32)
    qs = q.reshape(B, nb, qb, H, Dk).transpose(1, 0, 2, 3, 4)

    def block(qi):
        s = jnp.einsum('bqhd,bkhd->bhqk', qi.astype(jnp.float32), kf) * ATTN_SCALE
        p = jax.nn.softmax(s, axis=-1)
        return jnp.einsum('bhqk,bkhd->bqhd', p, vf).astype(q.dtype)

    o = lax.map(block, qs)
    return o.transpose(1, 0, 2, 3, 4).reshape(B, S, H * v.shape[-1])


def _fourier(h, w, b):
    B, S, D = h.shape
    hg = h.astype(jnp.float32).reshape(B, S, N_FGROUPS, FGROUP_DIM)
    f = jnp.fft.fft2(hg, axes=(1, 3), norm="ortho").real
    return f.reshape(B, S, D).astype(h.dtype) @ w + b


def setup_inputs(seed: int = 0) -> dict:
    key = jax.random.key(seed)
    ks = jax.random.split(key, 24)
    f32 = jnp.float32
    D = D_MODEL
    nrm = lambda k, shape, s: (jax.random.normal(k, shape, f32) * s)
    return {
        "x_prompt": nrm(ks[0], (BATCH, SEQ, D), 1.0),
        "x_sample": nrm(ks[1], (DEC_BATCH, DEC_SEQ, D), 1.0),
        "cache_ckv": nrm(ks[2], (DEC_BATCH, N_MLA, PAST_LEN, KV_LORA), 1.0),
        "cache_kpe": nrm(ks[3], (DEC_BATCH, N_MLA, PAST_LEN, QK_ROPE), 1.0),
        "c": nrm(ks[4], (DEC_BATCH, D), 1.0),
        "c_ctx": nrm(ks[5], (D,), 1.0),
        "w_mod": nrm(ks[6], (DEPTH, D, N_MOD * D), 0.5 * D ** -0.5),
        "b_mod": nrm(ks[7], (DEPTH, N_MOD * D), 0.01),
        "norm_g": 1.0 + nrm(ks[8], (DEPTH, 3, D), 0.05),
        "ffn_wg": nrm(ks[9], (DEPTH, 2, D, D_FF), D ** -0.5),
        "ffn_wu": nrm(ks[10], (DEPTH, 2, D, D_FF), D ** -0.5),
        "ffn_wd": nrm(ks[11], (DEPTH, 2, D_FF, D), D_FF ** -0.5),
        "mla_w_dq": nrm(ks[12], (N_MLA, D, Q_LORA), D ** -0.5),
        "mla_q_norm": 1.0 + nrm(ks[13], (N_MLA, Q_LORA), 0.05),
        "mla_w_uq": nrm(ks[14], (N_MLA, Q_LORA, N_HEADS * (QK_NOPE + QK_ROPE)), Q_LORA ** -0.5),
        "mla_w_dkv": nrm(ks[15], (N_MLA, D, KV_LORA + QK_ROPE), D ** -0.5),
        "mla_kv_norm": 1.0 + nrm(ks[16], (N_MLA, KV_LORA), 0.05),
        "mla_w_ukv": nrm(ks[17], (N_MLA, KV_LORA, N_HEADS * (QK_NOPE + V_DIM)), KV_LORA ** -0.5),
        "mla_w_o": nrm(ks[18], (N_MLA, N_HEADS * V_DIM, D), (N_HEADS * V_DIM) ** -0.5),
        "fourier_w": nrm(ks[19], (N_FOURIER, D, D), D ** -0.5),
        "fourier_b": nrm(ks[20], (N_FOURIER, D), 0.01),
        "final_norm": 1.0 + nrm(ks[21], (D,), 0.05),
    }


def reference(x_prompt, x_sample, cache_ckv, cache_kpe, c, c_ctx, w_mod, b_mod, norm_g,
              ffn_wg, ffn_wu, ffn_wd, mla_w_dq, mla_q_norm, mla_w_uq, mla_w_dkv, mla_kv_norm,
              mla_w_ukv, mla_w_o, fourier_w, fourier_b, final_norm):
    yp = x_prompt
    ys = x_sample
    rope_s = _axial_rope_tables(x_sample.shape[1])
    ckv_states = []
    kpe_states = []
    for l in range(DEPTH):
        mp = _modulation(c_ctx, w_mod[l], b_mod[l])
        ms = [m[:, None, :] for m in _modulation(c, w_mod[l], b_mod[l])]

        hp = _modulate(_rms(yp, norm_g[l, 0]), mp[0], mp[1])
        hs = _modulate(_rms(ys, norm_g[l, 0]), ms[0], ms[1])
        yp = yp + 0.5 * mp[2] * _swiglu(hp, ffn_wg[l, 0], ffn_wu[l, 0], ffn_wd[l, 0])
        ys = ys + 0.5 * ms[2] * _swiglu(hs, ffn_wg[l, 0], ffn_wu[l, 0], ffn_wd[l, 0])

        hp = _modulate(_rms(yp, norm_g[l, 1]), mp[3], mp[4])
        hs = _modulate(_rms(ys, norm_g[l, 1]), ms[3], ms[4])
        j = l // N_MIXERS
        if l % N_MIXERS == 0:
            qp, ckv_p, kpe_p = _mla_qkv(hp, mla_w_dq[j], mla_q_norm[j], mla_w_uq[j],
                                        mla_w_dkv[j], mla_kv_norm[j], None)
            kp, vp = _mla_up(ckv_p, kpe_p, mla_w_ukv[j])
            op = _attend(qp, kp, vp) @ mla_w_o[j]
            ckv_states.append(ckv_p)
            kpe_states.append(kpe_p)

            qs, ckv_s, kpe_s = _mla_qkv(hs, mla_w_dq[j], mla_q_norm[j], mla_w_uq[j],
                                        mla_w_dkv[j], mla_kv_norm[j], rope_s)
            k_ctx, v_ctx = _mla_up(cache_ckv[:, j], cache_kpe[:, j], mla_w_ukv[j])
            k_lat, v_lat = _mla_up(ckv_s, kpe_s, mla_w_ukv[j])
            k_all = jnp.concatenate([k_ctx, k_lat], axis=1)
            v_all = jnp.concatenate([v_ctx, v_lat], axis=1)
            os_ = _attend(qs, k_all, v_all) @ mla_w_o[j]
        else:
            op = _fourier(hp, fourier_w[j], fourier_b[j])
            os_ = _fourier(hs, fourier_w[j], fourier_b[j])
        yp = yp + mp[5] * op
        ys = ys + ms[5] * os_

        hp = _modulate(_rms(yp, norm_g[l, 2]), mp[6], mp[7])
        hs = _modulate(_rms(ys, norm_g[l, 2]), ms[6], ms[7])
        yp = yp + 0.5 * mp[8] * _swiglu(hp, ffn_wg[l, 1], ffn_wu[l, 1], ffn_wd[l, 1])
        ys = ys + 0.5 * ms[8] * _swiglu(hs, ffn_wg[l, 1], ffn_wu[l, 1], ffn_wd[l, 1])

    y_prompt = _rms(yp, final_norm)
    y_sample = _rms(ys, final_norm)
    new_ckv = jnp.stack(ckv_states, axis=1)
    new_kpe = jnp.stack(kpe_states, axis=1)
    return (y_prompt, y_sample, new_ckv, new_kpe)
```

```python
import functools
import math

import jax
import jax.numpy as jnp
from jax import lax
from jax.experimental import pallas as pl
from jax.experimental.pallas import tpu as pltpu

F32 = jnp.float32
BF16 = jnp.bfloat16

D_MODEL = 1024
BATCH = 32
SEQ = 256
DEPTH = 4
DEC_BATCH = 2
DEC_SEQ = 4096
PAST_LEN = 512
GRID_W = 64
N_HEADS = 8
QK_NOPE = 128
QK_ROPE = 64
V_DIM = 128
Q_LORA = 512
KV_LORA = 256
AXIS_FREQS = QK_ROPE // 4
ROPE_THETA = 10000.0
N_FGROUPS = 4
FGROUP_DIM = D_MODEL // N_FGROUPS
D_FF = 2816
N_MOD = 9
EPS = 1e-6
ATTN_SCALE = 1.0 / math.sqrt(QK_NOPE + QK_ROPE)

N_CTX = BATCH * SEQ
N_LAT = DEC_BATCH * DEC_SEQ
N_TOK = N_CTX + N_LAT
N_GROUPS = 8
HEAD_PAD = 256
ROPE_PAD = 128
VMEM_LIMIT = 56 * 1024 * 1024


def _cparams(sem):
    return pltpu.CompilerParams(dimension_semantics=sem, vmem_limit_bytes=VMEM_LIMIT)


def _group(i, tm):
    n_ctx = N_CTX // tm
    per_batch = DEC_SEQ // tm
    return jnp.where(i < n_ctx, 0, 1 + (i - n_ctx) // per_batch)


def _rms(x, g):
    return x * lax.rsqrt(jnp.mean(x * x, axis=-1, keepdims=True) + EPS) * g


def _mod_rms(x, g, mod, k):
    return _rms(x, g) * (1.0 + mod[k + 1:k + 2, :]) + mod[k:k + 1, :]


def _mod_kernel(c_ref, w_ref, b_ref, o_ref):
    c = c_ref[...]
    s = (c * jax.nn.sigmoid(c)).astype(BF16)
    o_ref[...] = jnp.dot(s, w_ref[...].astype(BF16), preferred_element_type=F32) + b_ref[...]


def _modulation(cond, w_mod, b_mod):
    out = pl.pallas_call(
        _mod_kernel,
        out_shape=jax.ShapeDtypeStruct((DEPTH, N_MOD, N_GROUPS, D_MODEL), F32),
        grid=(DEPTH, N_MOD),
        in_specs=[
            pl.BlockSpec((N_GROUPS, D_MODEL), lambda l, k: (0, 0)),
            pl.BlockSpec((None, D_MODEL, D_MODEL), lambda l, k: (l, 0, k)),
            pl.BlockSpec((None, None, 1, D_MODEL), lambda l, k: (l, k, 0, 0)),
        ],
        out_specs=pl.BlockSpec((None, None, N_GROUPS, D_MODEL), lambda l, k: (l, k, 0, 0)),
        compiler_params=_cparams(("arbitrary", "arbitrary")),
        name="adaln_modulation",
    )(cond, w_mod, b_mod.reshape(DEPTH, N_MOD, 1, D_MODEL))
    return out.transpose(0, 2, 1, 3)


FFN_TM = 1024
FFN_TF = 256


def _ffn_kernel(x_ref, mod_ref, g_ref, wg_ref, wu_ref, wd_ref, fn_ref, o_ref, h_ref, acc_ref,
                *, k0, final):
    j = pl.program_id(1)

    @pl.when(j == 0)
    def _():
        h_ref[...] = _mod_rms(x_ref[...], g_ref[...], mod_ref, k0).astype(BF16)
        acc_ref[...] = jnp.zeros_like(acc_ref)

    h = h_ref[...]
    g = jnp.dot(h, wg_ref[...], preferred_element_type=F32)
    u = jnp.dot(h, wu_ref[...], preferred_element_type=F32)
    a = (g * jax.nn.sigmoid(g) * u).astype(BF16)
    acc_ref[...] += jnp.dot(a, wd_ref[...], preferred_element_type=F32)

    @pl.when(j == pl.num_programs(1) - 1)
    def _():
        y = x_ref[...] + (0.5 * mod_ref[k0 + 2:k0 + 3, :]) * acc_ref[...]
        if final:
            y = _rms(y, fn_ref[...])
        o_ref[...] = y


def _ffn(x, modt, l, k0, g, wg, wu, wd, final_norm, final):
    tm, tf = FFN_TM, FFN_TF
    return pl.pallas_call(
        functools.partial(_ffn_kernel, k0=k0, final=final),
        out_shape=jax.ShapeDtypeStruct((N_TOK, D_MODEL), F32),
        grid=(N_TOK // tm, D_FF // tf),
        in_specs=[
            pl.BlockSpec((tm, D_MODEL), lambda i, j: (i, 0)),
            pl.BlockSpec((None, None, N_MOD, D_MODEL), lambda i, j: (l, _group(i, tm), 0, 0)),
            pl.BlockSpec((1, D_MODEL), lambda i, j: (0, 0)),
            pl.BlockSpec((D_MODEL, tf), lambda i, j: (0, j)),
            pl.BlockSpec((D_MODEL, tf), lambda i, j: (0, j)),
            pl.BlockSpec((tf, D_MODEL), lambda i, j: (j, 0)),
            pl.BlockSpec((1, D_MODEL), lambda i, j: (0, 0)),
        ],
        out_specs=pl.BlockSpec((tm, D_MODEL), lambda i, j: (i, 0)),
        scratch_shapes=[pltpu.VMEM((tm, D_MODEL), BF16), pltpu.VMEM((tm, D_MODEL), F32)],
        compiler_params=_cparams(("parallel", "arbitrary")),
        name="swiglu_halfstep",
    )(x, modt, g.reshape(1, D_MODEL), wg, wu, wd, final_norm.reshape(1, D_MODEL))


MLA_TM = 512


def _swap_halves(x):
    lane = lax.broadcasted_iota(jnp.int32, x.shape, 1)
    fwd = pltpu.roll(x, ROPE_PAD - AXIS_FREQS, axis=1)
    bwd = pltpu.roll(x, AXIS_FREQS, axis=1)
    return jnp.where(lane % (2 * AXIS_FREQS) < AXIS_FREQS, fwd, bwd)


def _rope(x, cos, sin):
    return x * cos + _swap_halves(x) * sin


def _kv_up(ckv, kpe_pad, wuk_ref, wuv_ref, k_ref, v_ref):
    c = ckv.astype(BF16)
    kn = jnp.dot(c, wuk_ref[...], preferred_element_type=F32).astype(BF16)
    v_ref[...] = jnp.dot(c, wuv_ref[...], preferred_element_type=F32).astype(BF16)
    kp = kpe_pad.astype(BF16)
    for hd in range(N_HEADS):
        k_ref[:, hd * HEAD_PAD:hd * HEAD_PAD + QK_NOPE] = kn[:, hd * QK_NOPE:(hd + 1) * QK_NOPE]
        k_ref[:, hd * HEAD_PAD + QK_NOPE:(hd + 1) * HEAD_PAD] = kp


def _mla_proj_kernel(x_ref, mod_ref, g_ref, wdq_ref, qn_ref, wuq_ref, wdkv_ref, kvn_ref,
                     wuk_ref, wuv_ref, cos_ref, sin_ref,
                     q_ref, k_ref, v_ref, ckv_ref, kpe_ref):
    h = _mod_rms(x_ref[...], g_ref[...], mod_ref, 3).astype(BF16)
    cq = _rms(jnp.dot(h, wdq_ref[...], preferred_element_type=F32), qn_ref[...]).astype(BF16)
    q = jnp.dot(cq, wuq_ref[...], preferred_element_type=F32)
    cos = cos_ref[...]
    sin = sin_ref[...]
    for hd in range(N_HEADS):
        lo = hd * HEAD_PAD
        q_ref[:, lo:lo + QK_NOPE] = (q[:, lo:lo + QK_NOPE] * ATTN_SCALE).astype(BF16)
        qr = _rope(q[:, lo + QK_NOPE:lo + HEAD_PAD], cos, sin)
        q_ref[:, lo + QK_NOPE:lo + HEAD_PAD] = (qr * ATTN_SCALE).astype(BF16)
    kv = jnp.dot(h, wdkv_ref[...], preferred_element_type=F32)
    ckv = _rms(kv[:, :KV_LORA], kvn_ref[...])
    kpe = _rope(kv[:, KV_LORA:], cos, sin)
    ckv_ref[...] = ckv
    kpe_ref[...] = kpe
    _kv_up(ckv, kpe, wuk_ref, wuv_ref, k_ref, v_ref)


def _mla_proj(x, modt, l, g, wdq, qn, wuq, wdkv, kvn, wuk, wuv, cos_t, sin_t):
    tm = MLA_TM
    n_ctx = N_CTX // tm
    per_batch = DEC_SEQ // tm

    def rope_idx(i):
        return jnp.where(i < n_ctx, per_batch, (i - n_ctx) % per_batch)

    full = lambda shape: pl.BlockSpec(shape, lambda i: (0, 0))
    rows = lambda w: pl.BlockSpec((tm, w), lambda i: (i, 0))
    return pl.pallas_call(
        _mla_proj_kernel,
        out_shape=(
            jax.ShapeDtypeStruct((N_TOK, N_HEADS * HEAD_PAD), BF16),
            jax.ShapeDtypeStruct((N_TOK, N_HEADS * HEAD_PAD), BF16),
            jax.ShapeDtypeStruct((N_TOK, N_HEADS * V_DIM), BF16),
            jax.ShapeDtypeStruct((N_TOK, KV_LORA), F32),
            jax.ShapeDtypeStruct((N_TOK, ROPE_PAD), F32),
        ),
        grid=(N_TOK // tm,),
        in_specs=[
            rows(D_MODEL),
            pl.BlockSpec((None, None, N_MOD, D_MODEL), lambda i: (l, _group(i, tm), 0, 0)),
            full((1, D_MODEL)),
            full((D_MODEL, Q_LORA)),
            full((1, Q_LORA)),
            full((Q_LORA, N_HEADS * HEAD_PAD)),
            full((D_MODEL, KV_LORA + ROPE_PAD)),
            full((1, KV_LORA)),
            full((KV_LORA, N_HEADS * QK_NOPE)),
            full((KV_LORA, N_HEADS * V_DIM)),
            pl.BlockSpec((tm, ROPE_PAD), lambda i: (rope_idx(i), 0)),
            pl.BlockSpec((tm, ROPE_PAD), lambda i: (rope_idx(i), 0)),
        ],
        out_specs=(rows(N_HEADS * HEAD_PAD), rows(N_HEADS * HEAD_PAD), rows(N_HEADS * V_DIM),
                   rows(KV_LORA), rows(ROPE_PAD)),
        compiler_params=_cparams(("parallel",)),
        name="mla_projections",
    )(x, modt, g.reshape(1, D_MODEL), wdq, qn.reshape(1, Q_LORA), wuq, wdkv,
      kvn.reshape(1, KV_LORA), wuk, wuv, cos_t, sin_t)


def _cache_kv_kernel(ckv_ref, kpe_ref, wuk_ref, wuv_ref, k_ref, v_ref):
    _kv_up(ckv_ref[...], kpe_ref[...], wuk_ref, wuv_ref, k_ref, v_ref)


def _cache_kv(ckv, kpe_pad, wuk, wuv):
    n = DEC_BATCH * PAST_LEN
    tm = PAST_LEN
    full = lambda shape: pl.BlockSpec(shape, lambda i: (0, 0))
    rows = lambda w: pl.BlockSpec((tm, w), lambda i: (i, 0))
    return pl.pallas_call(
        _cache_kv_kernel,
        out_shape=(jax.ShapeDtypeStruct((n, N_HEADS * HEAD_PAD), BF16),
                   jax.ShapeDtypeStruct((n, N_HEADS * V_DIM), BF16)),
        grid=(n // tm,),
        in_specs=[rows(KV_LORA), rows(ROPE_PAD),
                  full((KV_LORA, N_HEADS * QK_NOPE)), full((KV_LORA, N_HEADS * V_DIM))],
        out_specs=(rows(N_HEADS * HEAD_PAD), rows(N_HEADS * V_DIM)),
        compiler_params=_cparams(("parallel",)),
        name="cache_kv_up",
    )(ckv, kpe_pad, wuk, wuv)


_NT = (((1,), (1,)), ((), ()))


def _ctx_attn_kernel(q_ref, k_ref, v_ref, o_ref):
    for hd in range(N_HEADS):
        q = q_ref[:, hd * HEAD_PAD:(hd + 1) * HEAD_PAD]
        k = k_ref[:, hd * HEAD_PAD:(hd + 1) * HEAD_PAD]
        s = lax.dot_general(q, k, _NT, preferred_element_type=F32)
        p = jnp.exp(s - jnp.max(s, axis=-1, keepdims=True))
        o = jnp.dot(p.astype(BF16), v_ref[:, hd * V_DIM:(hd + 1) * V_DIM],
                    preferred_element_type=F32)
        o = o / jnp.sum(p, axis=-1, keepdims=True)
        o_ref[:, hd * V_DIM:(hd + 1) * V_DIM] = o.astype(BF16)


def _ctx_attn(q, k, v):
    rows = lambda w: pl.BlockSpec((SEQ, w), lambda b: (b, 0))
    return pl.pallas_call(
        _ctx_attn_kernel,
        out_shape=jax.ShapeDtypeStruct((N_TOK, N_HEADS * V_DIM), BF16),
        grid=(BATCH,),
        in_specs=[rows(N_HEADS * HEAD_PAD), rows(N_HEADS * HEAD_PAD), rows(N_HEADS * V_DIM)],
        out_specs=rows(N_HEADS * V_DIM),
        compiler_params=_cparams(("parallel",)),
        name="context_attention",
    )(q, k, v)


LAT_TQ = 512


def _lat_attn_kernel(q_ref, kc_ref, kl_ref, vc_ref, vl_ref, o_prev_ref, o_ref):
    del o_prev_ref
    q = q_ref[...]
    sc = lax.dot_general(q, kc_ref[...], _NT, preferred_element_type=F32)
    sl = lax.dot_general(q, kl_ref[...], _NT, preferred_element_type=F32)
    m = jnp.maximum(jnp.max(sc, axis=-1, keepdims=True), jnp.max(sl, axis=-1, keepdims=True))
    pc = jnp.exp(sc - m)
    pl_ = jnp.exp(sl - m)
    denom = jnp.sum(pc, axis=-1, keepdims=True) + jnp.sum(pl_, axis=-1, keepdims=True)
    o = jnp.dot(pc.astype(BF16), vc_ref[...], preferred_element_type=F32)
    o = o + jnp.dot(pl_.astype(BF16), vl_ref[...], preferred_element_type=F32)
    o_ref[...] = (o / denom).astype(BF16)


def _lat_attn(q, k, v, kc, vc, o_ctx):
    tq = LAT_TQ
    q_off = N_CTX // tq
    l_off = N_CTX // DEC_SEQ
    nq = DEC_SEQ // tq
    return pl.pallas_call(
        _lat_attn_kernel,
        out_shape=jax.ShapeDtypeStruct((N_TOK, N_HEADS * V_DIM), BF16),
        grid=(DEC_BATCH, N_HEADS, nq),
        in_specs=[
            pl.BlockSpec((tq, HEAD_PAD), lambda b, h, i: (q_off + b * nq + i, h)),
            pl.BlockSpec((PAST_LEN, HEAD_PAD), lambda b, h, i: (b, h)),
            pl.BlockSpec((DEC_SEQ, HEAD_PAD), lambda b, h, i: (l_off + b, h)),
            pl.BlockSpec((PAST_LEN, V_DIM), lambda b, h, i: (b, h)),
            pl.BlockSpec((DEC_SEQ, V_DIM), lambda b, h, i: (l_off + b, h)),
            pl.BlockSpec(memory_space=pl.ANY),
        ],
        out_specs=pl.BlockSpec((tq, V_DIM), lambda b, h, i: (q_off + b * nq + i, h)),
        input_output_aliases={5: 0},
        compiler_params=_cparams(("parallel", "parallel", "arbitrary")),
        name="latent_attention",
    )(q, kc, k, vc, v, o_ctx)


OPROJ_TM = 1024


def _oproj_kernel(x_ref, o_ref, w_ref, mod_ref, y_ref):
    y_ref[...] = x_ref[...] + mod_ref[5:6, :] * jnp.dot(o_ref[...], w_ref[...],
                                                       preferred_element_type=F32)


def _oproj(x, o, w, modt, l):
    tm = OPROJ_TM
    return pl.pallas_call(
        _oproj_kernel,
        out_shape=jax.ShapeDtypeStruct((N_TOK, D_MODEL), F32),
        grid=(N_TOK // tm,),
        in_specs=[
            pl.BlockSpec((tm, D_MODEL), lambda i: (i, 0)),
            pl.BlockSpec((tm, N_HEADS * V_DIM), lambda i: (i, 0)),
            pl.BlockSpec((N_HEADS * V_DIM, D_MODEL), lambda i: (0, 0)),
            pl.BlockSpec((None, None, N_MOD, D_MODEL), lambda i: (l, _group(i, tm), 0, 0)),
        ],
        out_specs=pl.BlockSpec((tm, D_MODEL), lambda i: (i, 0)),
        compiler_params=_cparams(("parallel",)),
        name="attn_out_proj",
    )(x, o, w, modt)


FCH_TM = 1024


def _fourier_ch_kernel(x_ref, mod_ref, g_ref, cs_ref, a_ref, b_ref):
    h = _mod_rms(x_ref[...], g_ref[...], mod_ref, 3).astype(BF16)
    for gi in range(N_FGROUPS):
        lo = gi * FGROUP_DIM
        r = jnp.dot(h[:, lo:lo + FGROUP_DIM], cs_ref[...], preferred_element_type=F32)
        a_ref[:, lo:lo + FGROUP_DIM] = r[:, :FGROUP_DIM].astype(BF16)
        b_ref[:, lo:lo + FGROUP_DIM] = r[:, FGROUP_DIM:].astype(BF16)


def _fourier_ch(x, modt, l, g, ch_tab):
    tm = FCH_TM
    rows = pl.BlockSpec((tm, D_MODEL), lambda i: (i, 0))
    return pl.pallas_call(
        _fourier_ch_kernel,
        out_shape=(jax.ShapeDtypeStruct((N_TOK, D_MODEL), BF16),
                   jax.ShapeDtypeStruct((N_TOK, D_MODEL), BF16)),
        grid=(N_TOK // tm,),
        in_specs=[
            rows,
            pl.BlockSpec((None, None, N_MOD, D_MODEL), lambda i: (l, _group(i, tm), 0, 0)),
            pl.BlockSpec((1, D_MODEL), lambda i: (0, 0)),
            pl.BlockSpec((FGROUP_DIM, 2 * FGROUP_DIM), lambda i: (0, 0)),
        ],
        out_specs=(rows, rows),
        compiler_params=_cparams(("parallel",)),
        name="fourier_channel_dft",
    )(x, modt, g.reshape(1, D_MODEL), ch_tab)


def _fourier_epilogue(f, x_ref, w_ref, bias_ref, mod_ref):
    mixed = jnp.dot(f.astype(BF16), w_ref[...], preferred_element_type=F32) + bias_ref[...]
    return x_ref[...] + mod_ref[5:6, :] * mixed


def _fourier_ctx_kernel(a_ref, b_ref, c_ref, s_ref, x_ref, w_ref, bias_ref, mod_ref, y_ref):
    f = jnp.dot(c_ref[...], a_ref[...], preferred_element_type=F32)
    f = f + jnp.dot(s_ref[...], b_ref[...], preferred_element_type=F32)
    y_ref[...] = _fourier_epilogue(f, x_ref, w_ref, bias_ref, mod_ref)


def _fourier_ctx(a, b, cos_t, nsin_t, x, w, bias, modt, l):
    rows = lambda dt: pl.BlockSpec((SEQ, D_MODEL), lambda i: (i, 0))
    full = lambda shape: pl.BlockSpec(shape, lambda i: (0, 0))
    return pl.pallas_call(
        _fourier_ctx_kernel,
        out_shape=jax.ShapeDtypeStruct((N_TOK, D_MODEL), F32),
        grid=(BATCH,),
        in_specs=[rows(BF16), rows(BF16), full((SEQ, SEQ)), full((SEQ, SEQ)), rows(F32),
                  full((D_MODEL, D_MODEL)), full((1, D_MODEL)),
                  pl.BlockSpec((None, None, N_MOD, D_MODEL), lambda i: (l, 0, 0, 0))],
        out_specs=rows(F32),
        compiler_params=_cparams(("parallel",)),
        name="fourier_context_positions",
    )(a, b, cos_t, nsin_t, x, w, bias.reshape(1, D_MODEL), modt)


FLAT_TM = 1024
FLAT_TK = 1024


def _fourier_lat_kernel(a_ref, b_ref, c_ref, s_ref, x_ref, w_ref, bias_ref, mod_ref, y_prev_ref,
                        y_ref, acc_ref):
    del y_prev_ref
    k = pl.program_id(2)

    @pl.when(k == 0)
    def _():
        acc_ref[...] = jnp.zeros_like(acc_ref)

    part = jnp.dot(c_ref[...], a_ref[...], preferred_element_type=F32)
    part = part + jnp.dot(s_ref[...], b_ref[...], preferred_element_type=F32)
    acc_ref[...] += part

    @pl.when(k == pl.num_programs(2) - 1)
    def _():
        y_ref[...] = _fourier_epilogue(acc_ref[...], x_ref, w_ref, bias_ref, mod_ref)


def _fourier_lat(a, b, cos_t, nsin_t, x, w, bias, modt, l, y_ctx):
    tm, tk = FLAT_TM, FLAT_TK
    ni = DEC_SEQ // tm
    nk = DEC_SEQ // tk
    r_off = N_CTX // tm
    k_off = N_CTX // tk
    src = pl.BlockSpec((tk, D_MODEL), lambda bb, i, k: (k_off + bb * nk + k, 0))
    tab = pl.BlockSpec((tm, tk), lambda bb, i, k: (i, k))
    xrow = pl.BlockSpec((tm, D_MODEL), lambda bb, i, k: (r_off + bb * ni + i, 0))
    return pl.pallas_call(
        _fourier_lat_kernel,
        out_shape=jax.ShapeDtypeStruct((N_TOK, D_MODEL), F32),
        grid=(DEC_BATCH, ni, nk),
        in_specs=[src, src, tab, tab, xrow,
                  pl.BlockSpec((D_MODEL, D_MODEL), lambda bb, i, k: (0, 0)),
                  pl.BlockSpec((1, D_MODEL), lambda bb, i, k: (0, 0)),
                  pl.BlockSpec((None, None, N_MOD, D_MODEL), lambda bb, i, k: (l, 1 + bb, 0, 0)),
                  pl.BlockSpec(memory_space=pl.ANY)],
        out_specs=xrow,
        scratch_shapes=[pltpu.VMEM((tm, D_MODEL), F32)],
        input_output_aliases={8: 0},
        compiler_params=_cparams(("parallel", "parallel", "arbitrary")),
        name="fourier_latent_positions",
    )(a, b, cos_t, nsin_t, x, w, bias.reshape(1, D_MODEL), modt, y_ctx)


def _dft_tables(n, scale):
    idx = jnp.arange(n, dtype=jnp.int32)
    ang = ((idx[:, None] * idx[None, :]) % n).astype(F32) * (2.0 * math.pi / n)
    return (jnp.cos(ang) * scale).astype(BF16), (jnp.sin(ang) * (-scale)).astype(BF16)


def _rope_tables(tm):
    rows = DEC_SEQ // GRID_W
    row = jnp.repeat(jnp.arange(rows, dtype=F32), GRID_W)
    col = jnp.tile(jnp.arange(GRID_W, dtype=F32), rows)
    inv = 1.0 / (ROPE_THETA ** (jnp.arange(AXIS_FREQS, dtype=F32) / AXIS_FREQS))
    ar = row[:, None] * inv
    ac = col[:, None] * inv
    pad = jnp.zeros((DEC_SEQ, ROPE_PAD - QK_ROPE), F32)
    cos = jnp.concatenate([jnp.cos(ar), jnp.cos(ar), jnp.cos(ac), jnp.cos(ac), pad + 1.0], axis=1)
    sin = jnp.concatenate([-jnp.sin(ar), jnp.sin(ar), -jnp.sin(ac), jnp.sin(ac), pad], axis=1)
    cos = jnp.concatenate([cos, jnp.ones((tm, ROPE_PAD), F32)], axis=0)
    sin = jnp.concatenate([sin, jnp.zeros((tm, ROPE_PAD), F32)], axis=0)
    return cos, sin


def _pad_heads(w, part, width):
    k = w.shape[0]
    w = w.reshape(k, N_HEADS, -1)[:, :, part]
    return jnp.pad(w, ((0, 0), (0, 0), (0, width - w.shape[-1])))


def kernel(x_prompt, x_sample, cache_ckv, cache_kpe, c, c_ctx, w_mod, b_mod, norm_g, ffn_wg, ffn_wu,
           ffn_wd, mla_w_dq, mla_q_norm, mla_w_uq, mla_w_dkv, mla_kv_norm, mla_w_ukv, mla_w_o,
           fourier_w, fourier_b, final_norm):
    x = jnp.concatenate([x_prompt.reshape(N_CTX, D_MODEL), x_sample.reshape(N_LAT, D_MODEL)], axis=0)
    cond = jnp.concatenate([c_ctx[None, :], c, jnp.zeros((N_GROUPS - 1 - DEC_BATCH, D_MODEL), F32)], axis=0)
    modt = _modulation(cond, w_mod, b_mod)

    wg = ffn_wg.astype(BF16)
    wu = ffn_wu.astype(BF16)
    wd = ffn_wd.astype(BF16)

    rope_cos, rope_sin = _rope_tables(MLA_TM)
    ch_cos, ch_nsin = _dft_tables(FGROUP_DIM, 1.0 / math.sqrt(FGROUP_DIM))
    ch_tab = jnp.concatenate([ch_cos, -ch_nsin], axis=1)
    ctx_cos, ctx_nsin = _dft_tables(SEQ, 1.0 / math.sqrt(SEQ))
    lat_cos, lat_nsin = _dft_tables(DEC_SEQ, 1.0 / math.sqrt(DEC_SEQ))

    ckv_states = []
    kpe_states = []
    for l in range(DEPTH):
        j = l // 2
        x = _ffn(x, modt, l, 0, norm_g[l, 0], wg[l, 0], wu[l, 0], wd[l, 0], final_norm, False)
        if l % 2 == 0:
            wuq = mla_w_uq[j]
            wuq_p = jnp.concatenate(
                [_pad_heads(wuq, slice(0, QK_NOPE), QK_NOPE),
                 _pad_heads(wuq, slice(QK_NOPE, QK_NOPE + QK_ROPE), ROPE_PAD)], axis=-1)
            wuq_p = wuq_p.reshape(Q_LORA, N_HEADS * HEAD_PAD).astype(BF16)
            wdkv_p = jnp.pad(mla_w_dkv[j], ((0, 0), (0, ROPE_PAD - QK_ROPE))).astype(BF16)
            wukv = mla_w_ukv[j]
            wuk = _pad_heads(wukv, slice(0, QK_NOPE), QK_NOPE).reshape(KV_LORA, -1).astype(BF16)
            wuv = _pad_heads(wukv, slice(QK_NOPE, QK_NOPE + V_DIM), V_DIM).reshape(KV_LORA, -1).astype(BF16)
            q, k, v, ckv, kpe = _mla_proj(
                x, modt, l, norm_g[l, 1], mla_w_dq[j].astype(BF16), mla_q_norm[j], wuq_p, wdkv_p,
                mla_kv_norm[j], wuk, wuv, rope_cos, rope_sin)
            kc, vc = _cache_kv(
                cache_ckv[:, j].reshape(DEC_BATCH * PAST_LEN, KV_LORA),
                jnp.pad(cache_kpe[:, j].reshape(DEC_BATCH * PAST_LEN, QK_ROPE),
                        ((0, 0), (0, ROPE_PAD - QK_ROPE))),
                wuk, wuv)
            o = _ctx_attn(q, k, v)
            o = _lat_attn(q, k, v, kc, vc, o)
            x = _oproj(x, o, mla_w_o[j].astype(BF16), modt, l)
            ckv_states.append(ckv[:N_CTX].reshape(BATCH, SEQ, KV_LORA))
            kpe_states.append(kpe[:N_CTX, :QK_ROPE].reshape(BATCH, SEQ, QK_ROPE))
        else:
            a, b = _fourier_ch(x, modt, l, norm_g[l, 1], ch_tab)
            fw = fourier_w[j].astype(BF16)
            y = _fourier_ctx(a, b, ctx_cos, ctx_nsin, x, fw, fourier_b[j], modt, l)
            x = _fourier_lat(a, b, lat_cos, lat_nsin, x, fw, fourier_b[j], modt, l, y)
        x = _ffn(x, modt, l, 6, norm_g[l, 2], wg[l, 1], wu[l, 1], wd[l, 1], final_norm,
                 l == DEPTH - 1)

    y_prompt = x[:N_CTX].reshape(BATCH, SEQ, D_MODEL)
    y_sample = x[N_CTX:].reshape(DEC_BATCH, DEC_SEQ, D_MODEL)
    new_ckv = jnp.stack(ckv_states, axis=1)
    new_kpe = jnp.stack(kpe_states, axis=1)
    return (y_prompt, y_sample, new_ckv, new_kpe)
```

```python
import functools
import math

import jax
import jax.numpy as jnp
from jax import lax
from jax.experimental import pallas as pl
from jax.experimental.pallas import tpu as pltpu

F32 = jnp.float32
BF16 = jnp.bfloat16

D_MODEL = 1024
BATCH = 32
SEQ = 256
DEPTH = 4
DEC_BATCH = 2
DEC_SEQ = 4096
PAST_LEN = 512
GRID_W = 64
N_HEADS = 8
QK_NOPE = 128
QK_ROPE = 64
V_DIM = 128
Q_LORA = 512
KV_LORA = 256
AXIS_FREQS = QK_ROPE // 4
ROPE_THETA = 10000.0
N_FGROUPS = 4
FGROUP_DIM = D_MODEL // N_FGROUPS
D_FF = 2816
N_MOD = 9
EPS = 1e-6
ATTN_SCALE = 1.0 / math.sqrt(QK_NOPE + QK_ROPE)
Q_SCALE = ATTN_SCALE * math.log2(math.e)

N_CTX = BATCH * SEQ
N_LAT = DEC_BATCH * DEC_SEQ
N_TOK = N_CTX + N_LAT
N_GROUPS = 8
HEAD_PAD = 256
ROPE_PAD = 128
VMEM_LIMIT = 56 * 1024 * 1024


def _cparams(sem):
    return pltpu.CompilerParams(dimension_semantics=sem, vmem_limit_bytes=VMEM_LIMIT)


def _group(i, tm):
    n_ctx = N_CTX // tm
    per_batch = DEC_SEQ // tm
    return jnp.where(i < n_ctx, 0, 1 + (i - n_ctx) // per_batch)


def _rms(x, g):
    return x * lax.rsqrt(jnp.mean(x * x, axis=-1, keepdims=True) + EPS) * g


def _mod_rms(x, g, mod, k):
    return _rms(x, g) * (1.0 + mod[k + 1:k + 2, :]) + mod[k:k + 1, :]


def _mod_kernel(c_ref, w_ref, b_ref, o_ref):
    c = c_ref[...]
    s = (c * jax.nn.sigmoid(c)).astype(BF16)
    o_ref[...] = jnp.dot(s, w_ref[...].astype(BF16), preferred_element_type=F32) + b_ref[...]


def _modulation(cond, w_mod, b_mod):
    out = pl.pallas_call(
        _mod_kernel,
        out_shape=jax.ShapeDtypeStruct((DEPTH, N_MOD, N_GROUPS, D_MODEL), F32),
        grid=(DEPTH, N_MOD),
        in_specs=[
            pl.BlockSpec((N_GROUPS, D_MODEL), lambda l, k: (0, 0)),
            pl.BlockSpec((None, D_MODEL, D_MODEL), lambda l, k: (l, 0, k)),
            pl.BlockSpec((None, None, 1, D_MODEL), lambda l, k: (l, k, 0, 0)),
        ],
        out_specs=pl.BlockSpec((None, None, N_GROUPS, D_MODEL), lambda l, k: (l, k, 0, 0)),
        compiler_params=_cparams(("arbitrary", "arbitrary")),
        name="adaln_modulation",
    )(cond, w_mod, b_mod.reshape(DEPTH, N_MOD, 1, D_MODEL))
    return out.transpose(0, 2, 1, 3)


FFN_TM = 512
FFN_TF = 256
FFN_CTX_TILES = N_CTX // FFN_TM


def _ffn_kernel(*refs, k0, split_in, final):
    n_x = 2 if split_in else 1
    x_refs, (mod_ref, g_ref, wg_ref, wu_ref, wd_ref, fn_ref) = refs[:n_x], refs[n_x:n_x + 6]
    n_out = 2 if final else 1
    out_refs, (a_ref,) = refs[n_x + 6:n_x + 6 + n_out], refs[n_x + 6 + n_out:]
    is_ctx = pl.program_id(0) < FFN_CTX_TILES

    if split_in:
        x = jnp.where(is_ctx, x_refs[0][...], x_refs[1][...])
    else:
        x = x_refs[0][...]
    h = _mod_rms(x, g_ref[...], mod_ref, k0).astype(BF16)
    for c in range(D_FF // FFN_TF):
        cols = slice(c * FFN_TF, (c + 1) * FFN_TF)
        g = jnp.dot(h, wg_ref[:, cols], preferred_element_type=F32)
        u = jnp.dot(h, wu_ref[:, cols], preferred_element_type=F32)
        a_ref[:, cols] = (g * jax.nn.sigmoid(g) * u).astype(BF16)
    y = jnp.dot(a_ref[...], wd_ref[...], preferred_element_type=F32)
    y = x + (0.5 * mod_ref[k0 + 2:k0 + 3, :]) * y
    if final:
        y = _rms(y, fn_ref[...])

        @pl.when(is_ctx)
        def _():
            out_refs[0][...] = y

        @pl.when(jnp.logical_not(is_ctx))
        def _():
            out_refs[1][...] = y
    else:
        out_refs[0][...] = y


def _ffn(xs, modt, l, k0, g, wg, wu, wd, final_norm, final):
    tm = FFN_TM
    split_in = len(xs) == 2
    rows = pl.BlockSpec((tm, D_MODEL), lambda i: (i, 0))
    ctx_rows = pl.BlockSpec((tm, D_MODEL), lambda i: (jnp.minimum(i, FFN_CTX_TILES - 1), 0))
    lat_rows = pl.BlockSpec((tm, D_MODEL), lambda i: (jnp.maximum(i - FFN_CTX_TILES, 0), 0))
    resident = lambda shape: pl.BlockSpec(shape, lambda i: (0, 0), pipeline_mode=pl.Buffered(1))
    if final:
        out_shape = (jax.ShapeDtypeStruct((N_CTX, D_MODEL), F32),
                     jax.ShapeDtypeStruct((N_LAT, D_MODEL), F32))
        out_specs = (ctx_rows, lat_rows)
    else:
        out_shape = jax.ShapeDtypeStruct((N_TOK, D_MODEL), F32)
        out_specs = rows
    return pl.pallas_call(
        functools.partial(_ffn_kernel, k0=k0, split_in=split_in, final=final),
        out_shape=out_shape,
        grid=(N_TOK // tm,),
        in_specs=([ctx_rows, lat_rows] if split_in else [rows]) + [
            pl.BlockSpec((None, None, N_MOD, D_MODEL), lambda i: (l, _group(i, tm), 0, 0)),
            pl.BlockSpec((1, D_MODEL), lambda i: (0, 0)),
            resident((D_MODEL, D_FF)),
            resident((D_MODEL, D_FF)),
            resident((D_FF, D_MODEL)),
            pl.BlockSpec((1, D_MODEL), lambda i: (0, 0)),
        ],
        out_specs=out_specs,
        scratch_shapes=[pltpu.VMEM((tm, D_FF), BF16)],
        compiler_params=_cparams(("arbitrary",)),
        name="swiglu_halfstep",
    )(*xs, modt, g.reshape(1, D_MODEL), wg, wu, wd, final_norm.reshape(1, D_MODEL))


MLA_TM = 512


def _swap_halves(x):
    lane = lax.broadcasted_iota(jnp.int32, x.shape, 1)
    fwd = pltpu.roll(x, ROPE_PAD - AXIS_FREQS, axis=1)
    bwd = pltpu.roll(x, AXIS_FREQS, axis=1)
    return jnp.where(lane % (2 * AXIS_FREQS) < AXIS_FREQS, fwd, bwd)


def _rope(x, cos, sin):
    return x * cos + _swap_halves(x) * sin


def _kv_up(ckv, kpe_pad, wuk_ref, wuvt_ref, k_ref, v_ref, vt_ref):
    c = ckv.astype(BF16)
    kn = jnp.dot(c, wuk_ref[...], preferred_element_type=F32).astype(BF16)
    vt = lax.dot_general(wuvt_ref[...], c, _NT, preferred_element_type=F32)
    vt_ref[...] = vt.astype(BF16)
    if v_ref is not None:
        v_ref[...] = vt.T.astype(BF16)
    kp = kpe_pad.astype(BF16)
    for hd in range(N_HEADS):
        k_ref[:, hd * HEAD_PAD:hd * HEAD_PAD + QK_NOPE] = kn[:, hd * QK_NOPE:(hd + 1) * QK_NOPE]
        k_ref[:, hd * HEAD_PAD + QK_NOPE:(hd + 1) * HEAD_PAD] = kp


def _mla_proj_kernel(*refs, has_prev):
    (x_ref, mod_ref, g_ref, wdq_ref, qn_ref, wuq_ref, wdkv_ref, kvn_ref, wuk_ref, wuvt_ref,
     cos_ref, sin_ref) = refs[:12]
    q_ref, k_ref, v_ref, vt_ref, ckv_ref, kpe_ref = refs[12 + (2 if has_prev else 0):]
    h = _mod_rms(x_ref[...], g_ref[...], mod_ref, 3).astype(BF16)
    cq = _rms(jnp.dot(h, wdq_ref[...], preferred_element_type=F32), qn_ref[...]).astype(BF16)
    q = jnp.dot(cq, wuq_ref[...], preferred_element_type=F32)
    cos = cos_ref[...]
    sin = sin_ref[...]
    for hd in range(N_HEADS):
        lo = hd * HEAD_PAD
        q_ref[:, lo:lo + QK_NOPE] = (q[:, lo:lo + QK_NOPE] * Q_SCALE).astype(BF16)
        qr = _rope(q[:, lo + QK_NOPE:lo + HEAD_PAD], cos, sin)
        q_ref[:, lo + QK_NOPE:lo + HEAD_PAD] = (qr * Q_SCALE).astype(BF16)
    kv = jnp.dot(h, wdkv_ref[...], preferred_element_type=F32)
    ckv = _rms(kv[:, :KV_LORA], kvn_ref[...])
    kpe = _rope(kv[:, KV_LORA:], cos, sin)

    @pl.when(pl.program_id(0) < N_CTX // MLA_TM)
    def _():
        ckv_ref[...] = ckv.reshape(ckv_ref.shape)
        kpe_ref[...] = kpe[:, :QK_ROPE].reshape(kpe_ref.shape)

    _kv_up(ckv, kpe, wuk_ref, wuvt_ref, k_ref, v_ref, vt_ref)


def _mla_proj(x, modt, l, g, wdq, qn, wuq, wdkv, kvn, wuk, wuvt, cos_t, sin_t, prev_states):
    tm = MLA_TM
    n_ctx = N_CTX // tm
    per_batch = DEC_SEQ // tm
    slot = l // 2
    n_mla = (DEPTH + 1) // 2

    def rope_idx(i):
        return jnp.where(i < n_ctx, per_batch, (i - n_ctx) % per_batch)

    full = lambda shape: pl.BlockSpec(shape, lambda i: (0, 0))
    rows = lambda w: pl.BlockSpec((tm, w), lambda i: (i, 0))
    state = lambda w: pl.BlockSpec((tm // SEQ, None, SEQ, w),
                                   lambda i: (jnp.minimum(i, n_ctx - 1), slot, 0, 0))
    has_prev = prev_states is not None
    prev_specs = [pl.BlockSpec(memory_space=pl.ANY)] * 2 if has_prev else []
    return pl.pallas_call(
        functools.partial(_mla_proj_kernel, has_prev=has_prev),
        out_shape=(
            jax.ShapeDtypeStruct((N_TOK, N_HEADS * HEAD_PAD), BF16),
            jax.ShapeDtypeStruct((N_TOK, N_HEADS * HEAD_PAD), BF16),
            jax.ShapeDtypeStruct((N_TOK, N_HEADS * V_DIM), BF16),
            jax.ShapeDtypeStruct((N_HEADS * V_DIM, N_TOK), BF16),
            jax.ShapeDtypeStruct((BATCH, n_mla, SEQ, KV_LORA), F32),
            jax.ShapeDtypeStruct((BATCH, n_mla, SEQ, QK_ROPE), F32),
        ),
        grid=(N_TOK // tm,),
        in_specs=[
            rows(D_MODEL),
            pl.BlockSpec((None, None, N_MOD, D_MODEL), lambda i: (l, _group(i, tm), 0, 0)),
            full((1, D_MODEL)),
            full((D_MODEL, Q_LORA)),
            full((1, Q_LORA)),
            full((Q_LORA, N_HEADS * HEAD_PAD)),
            full((D_MODEL, KV_LORA + ROPE_PAD)),
            full((1, KV_LORA)),
            full((KV_LORA, N_HEADS * QK_NOPE)),
            full((N_HEADS * V_DIM, KV_LORA)),
            pl.BlockSpec((tm, ROPE_PAD), lambda i: (rope_idx(i), 0)),
            pl.BlockSpec((tm, ROPE_PAD), lambda i: (rope_idx(i), 0)),
        ] + prev_specs,
        out_specs=(rows(N_HEADS * HEAD_PAD), rows(N_HEADS * HEAD_PAD), rows(N_HEADS * V_DIM),
                   pl.BlockSpec((N_HEADS * V_DIM, tm), lambda i: (0, i)),
                   state(KV_LORA), state(QK_ROPE)),
        input_output_aliases={12: 4, 13: 5} if has_prev else {},
        compiler_params=_cparams(("arbitrary",)),
        name="mla_projections",
    )(x, modt, g.reshape(1, D_MODEL), wdq, qn.reshape(1, Q_LORA), wuq, wdkv,
      kvn.reshape(1, KV_LORA), wuk, wuvt, cos_t, sin_t, *(prev_states or ()))


def _cache_kv_kernel(ckv_ref, kpe_ref, wuk_ref, wuvt_ref, k_ref, vt_ref):
    _kv_up(ckv_ref[...], kpe_ref[...], wuk_ref, wuvt_ref, k_ref, None, vt_ref)


def _cache_kv(ckv, kpe_pad, wuk, wuvt):
    n = DEC_BATCH * PAST_LEN
    tm = PAST_LEN
    full = lambda shape: pl.BlockSpec(shape, lambda i: (0, 0))
    rows = lambda w: pl.BlockSpec((tm, w), lambda i: (i, 0))
    return pl.pallas_call(
        _cache_kv_kernel,
        out_shape=(jax.ShapeDtypeStruct((n, N_HEADS * HEAD_PAD), BF16),
                   jax.ShapeDtypeStruct((N_HEADS * V_DIM, n), BF16)),
        grid=(n // tm,),
        in_specs=[rows(KV_LORA), rows(ROPE_PAD),
                  full((KV_LORA, N_HEADS * QK_NOPE)), full((N_HEADS * V_DIM, KV_LORA))],
        out_specs=(rows(N_HEADS * HEAD_PAD), pl.BlockSpec((N_HEADS * V_DIM, tm), lambda i: (0, i))),
        compiler_params=_cparams(("parallel",)),
        name="cache_kv_up",
    )(ckv, kpe_pad, wuk, wuvt)


_NT = (((1,), (1,)), ((), ()))


def _ctx_attn_kernel(q_ref, k_ref, v_ref, o_ref):
    for hd in range(N_HEADS):
        q = q_ref[:, hd * HEAD_PAD:(hd + 1) * HEAD_PAD]
        k = k_ref[:, hd * HEAD_PAD:(hd + 1) * HEAD_PAD]
        s = lax.dot_general(q, k, _NT, preferred_element_type=F32)
        p = jnp.exp2(s - jnp.max(s, axis=-1, keepdims=True))
        o = jnp.dot(p.astype(BF16), v_ref[:, hd * V_DIM:(hd + 1) * V_DIM],
                    preferred_element_type=F32)
        o = o / jnp.sum(p, axis=-1, keepdims=True)
        o_ref[:, hd * V_DIM:(hd + 1) * V_DIM] = o.astype(BF16)


def _ctx_attn(q, k, v):
    rows = lambda w: pl.BlockSpec((SEQ, w), lambda b: (b, 0))
    return pl.pallas_call(
        _ctx_attn_kernel,
        out_shape=jax.ShapeDtypeStruct((N_TOK, N_HEADS * V_DIM), BF16),
        grid=(BATCH,),
        in_specs=[rows(N_HEADS * HEAD_PAD), rows(N_HEADS * HEAD_PAD), rows(N_HEADS * V_DIM)],
        out_specs=rows(N_HEADS * V_DIM),
        compiler_params=_cparams(("parallel",)),
        name="context_attention",
    )(q, k, v)


LAT_TQ = 512
LAT_KC = 512
LAT_KEYS = PAST_LEN + DEC_SEQ
SUBLANES = 8


def _lat_attn_kernel(q_ref, kc_ref, kl_ref, vct_ref, vlt_ref, o_prev_ref, o_ref, s_ref):
    del o_prev_ref
    tq = q_ref.shape[0]
    kc = LAT_KC
    n_cache = PAST_LEN // kc
    n_chunks = LAT_KEYS // kc

    def rows8(x, op):
        return op(x.reshape(kc // SUBLANES, SUBLANES, tq), axis=0)

    q = q_ref[...]
    mpart = None
    for c in range(n_chunks):
        if c < n_cache:
            k = kc_ref[c * kc:(c + 1) * kc, :]
        else:
            k = kl_ref[(c - n_cache) * kc:(c - n_cache + 1) * kc, :]
        st = lax.dot_general(k, q, _NT, preferred_element_type=F32)
        s_ref[c * kc:(c + 1) * kc, :] = st
        cm = rows8(st, jnp.max)
        mpart = cm if mpart is None else jnp.maximum(mpart, cm)
    m = jnp.max(mpart, axis=0, keepdims=True)

    lpart = jnp.zeros((SUBLANES, tq), F32)
    acc = jnp.zeros((V_DIM, tq), F32)
    for c in range(n_chunks):
        if c < n_cache:
            vt = vct_ref[:, c * kc:(c + 1) * kc]
        else:
            vt = vlt_ref[:, (c - n_cache) * kc:(c - n_cache + 1) * kc]
        p = jnp.exp2(s_ref[c * kc:(c + 1) * kc, :] - m)
        lpart = lpart + rows8(p, jnp.sum)
        acc = acc + jnp.dot(vt, p.astype(BF16), preferred_element_type=F32)
    denom = jnp.sum(lpart, axis=0, keepdims=True)
    o_ref[...] = (acc / denom).T.astype(BF16)


def _lat_attn(q, k, vt, kc, vct, o_ctx):
    tq = LAT_TQ
    q_off = N_CTX // tq
    l_off = N_CTX // DEC_SEQ
    nq = DEC_SEQ // tq
    return pl.pallas_call(
        _lat_attn_kernel,
        out_shape=jax.ShapeDtypeStruct((N_TOK, N_HEADS * V_DIM), BF16),
        grid=(DEC_BATCH, N_HEADS, nq),
        in_specs=[
            pl.BlockSpec((tq, HEAD_PAD), lambda b, h, i: (q_off + b * nq + i, h)),
            pl.BlockSpec((PAST_LEN, HEAD_PAD), lambda b, h, i: (b, h)),
            pl.BlockSpec((DEC_SEQ, HEAD_PAD), lambda b, h, i: (l_off + b, h)),
            pl.BlockSpec((V_DIM, PAST_LEN), lambda b, h, i: (h, b)),
            pl.BlockSpec((V_DIM, DEC_SEQ), lambda b, h, i: (h, l_off + b)),
            pl.BlockSpec(memory_space=pl.ANY),
        ],
        out_specs=pl.BlockSpec((tq, V_DIM), lambda b, h, i: (q_off + b * nq + i, h)),
        scratch_shapes=[pltpu.VMEM((LAT_KEYS, tq), F32)],
        input_output_aliases={5: 0},
        compiler_params=_cparams(("arbitrary", "arbitrary", "arbitrary")),
        name="latent_attention",
    )(q, kc, k, vct, vt, o_ctx)


OPROJ_TM = 1024


def _oproj_kernel(x_ref, o_ref, w_ref, mod_ref, y_ref):
    y_ref[...] = x_ref[...] + mod_ref[5:6, :] * jnp.dot(o_ref[...], w_ref[...],
                                                       preferred_element_type=F32)


def _oproj(x, o, w, modt, l):
    tm = OPROJ_TM
    return pl.pallas_call(
        _oproj_kernel,
        out_shape=jax.ShapeDtypeStruct((N_TOK, D_MODEL), F32),
        grid=(N_TOK // tm,),
        in_specs=[
            pl.BlockSpec((tm, D_MODEL), lambda i: (i, 0)),
            pl.BlockSpec((tm, N_HEADS * V_DIM), lambda i: (i, 0)),
            pl.BlockSpec((N_HEADS * V_DIM, D_MODEL), lambda i: (0, 0)),
            pl.BlockSpec((None, None, N_MOD, D_MODEL), lambda i: (l, _group(i, tm), 0, 0)),
        ],
        out_specs=pl.BlockSpec((tm, D_MODEL), lambda i: (i, 0)),
        compiler_params=_cparams(("parallel",)),
        name="attn_out_proj",
    )(x, o, w, modt)


FCH_TM = 1024


def _fourier_ch_kernel(x_ref, mod_ref, g_ref, cs_ref, a_ref, b_ref):
    h = _mod_rms(x_ref[...], g_ref[...], mod_ref, 3).astype(BF16)
    for gi in range(N_FGROUPS):
        lo = gi * FGROUP_DIM
        r = jnp.dot(h[:, lo:lo + FGROUP_DIM], cs_ref[...], preferred_element_type=F32)
        a_ref[:, lo:lo + FGROUP_DIM] = r[:, :FGROUP_DIM].astype(BF16)
        b_ref[:, lo:lo + FGROUP_DIM] = r[:, FGROUP_DIM:].astype(BF16)


def _fourier_ch(x, modt, l, g, ch_tab):
    tm = FCH_TM
    rows = pl.BlockSpec((tm, D_MODEL), lambda i: (i, 0))
    return pl.pallas_call(
        _fourier_ch_kernel,
        out_shape=(jax.ShapeDtypeStruct((N_TOK, D_MODEL), BF16),
                   jax.ShapeDtypeStruct((N_TOK, D_MODEL), BF16)),
        grid=(N_TOK // tm,),
        in_specs=[
            rows,
            pl.BlockSpec((None, None, N_MOD, D_MODEL), lambda i: (l, _group(i, tm), 0, 0)),
            pl.BlockSpec((1, D_MODEL), lambda i: (0, 0)),
            pl.BlockSpec((FGROUP_DIM, 2 * FGROUP_DIM), lambda i: (0, 0)),
        ],
        out_specs=(rows, rows),
        compiler_params=_cparams(("parallel",)),
        name="fourier_channel_dft",
    )(x, modt, g.reshape(1, D_MODEL), ch_tab)


def _fourier_epilogue(f, x_ref, w_ref, bias_ref, mod_ref):
    mixed = jnp.dot(f.astype(BF16), w_ref[...], preferred_element_type=F32) + bias_ref[...]
    return x_ref[...] + mod_ref[5:6, :] * mixed


def _fourier_ctx_kernel(a_ref, b_ref, c_ref, s_ref, x_ref, w_ref, bias_ref, mod_ref, y_ref):
    f = jnp.dot(c_ref[...], a_ref[...], preferred_element_type=F32)
    f = f + jnp.dot(s_ref[...], b_ref[...], preferred_element_type=F32)
    y_ref[...] = _fourier_epilogue(f, x_ref, w_ref, bias_ref, mod_ref)


def _fourier_ctx(a, b, cos_t, nsin_t, x, w, bias, modt, l):
    rows = lambda dt: pl.BlockSpec((SEQ, D_MODEL), lambda i: (i, 0))
    full = lambda shape: pl.BlockSpec(shape, lambda i: (0, 0))
    return pl.pallas_call(
        _fourier_ctx_kernel,
        out_shape=jax.ShapeDtypeStruct((N_TOK, D_MODEL), F32),
        grid=(BATCH,),
        in_specs=[rows(BF16), rows(BF16), full((SEQ, SEQ)), full((SEQ, SEQ)), rows(F32),
                  full((D_MODEL, D_MODEL)), full((1, D_MODEL)),
                  pl.BlockSpec((None, None, N_MOD, D_MODEL), lambda i: (l, 0, 0, 0))],
        out_specs=rows(F32),
        compiler_params=_cparams(("parallel",)),
        name="fourier_context_positions",
    )(a, b, cos_t, nsin_t, x, w, bias.reshape(1, D_MODEL), modt)


FLAT_TM = 1024
FLAT_TK = 1024


def _fourier_lat_kernel(a_ref, b_ref, c_ref, s_ref, x_ref, w_ref, bias_ref, mod_ref, y_prev_ref,
                        y_ref, acc_ref):
    del y_prev_ref
    k = pl.program_id(2)

    @pl.when(k == 0)
    def _():
        acc_ref[...] = jnp.zeros_like(acc_ref)

    part = jnp.dot(c_ref[...], a_ref[...], preferred_element_type=F32)
    part = part + jnp.dot(s_ref[...], b_ref[...], preferred_element_type=F32)
    acc_ref[...] += part

    @pl.when(k == pl.num_programs(2) - 1)
    def _():
        y_ref[...] = _fourier_epilogue(acc_ref[...], x_ref, w_ref, bias_ref, mod_ref)


def _fourier_lat(a, b, cos_t, nsin_t, x, w, bias, modt, l, y_ctx):
    tm, tk = FLAT_TM, FLAT_TK
    ni = DEC_SEQ // tm
    nk = DEC_SEQ // tk
    r_off = N_CTX // tm
    k_off = N_CTX // tk
    src = pl.BlockSpec((tk, D_MODEL), lambda bb, i, k: (k_off + bb * nk + k, 0))
    tab = pl.BlockSpec((tm, tk), lambda bb, i, k: (i, k))
    xrow = pl.BlockSpec((tm, D_MODEL), lambda bb, i, k: (r_off + bb * ni + i, 0))
    return pl.pallas_call(
        _fourier_lat_kernel,
        out_shape=jax.ShapeDtypeStruct((N_TOK, D_MODEL), F32),
        grid=(DEC_BATCH, ni, nk),
        in_specs=[src, src, tab, tab, xrow,
                  pl.BlockSpec((D_MODEL, D_MODEL), lambda bb, i, k: (0, 0)),
                  pl.BlockSpec((1, D_MODEL), lambda bb, i, k: (0, 0)),
                  pl.BlockSpec((None, None, N_MOD, D_MODEL), lambda bb, i, k: (l, 1 + bb, 0, 0)),
                  pl.BlockSpec(memory_space=pl.ANY)],
        out_specs=xrow,
        scratch_shapes=[pltpu.VMEM((tm, D_MODEL), F32)],
        input_output_aliases={8: 0},
        compiler_params=_cparams(("parallel", "parallel", "arbitrary")),
        name="fourier_latent_positions",
    )(a, b, cos_t, nsin_t, x, w, bias.reshape(1, D_MODEL), modt, y_ctx)


TAB_R = 64
TAB_TM = 256


def _dft_table_kernel(ar_ref, ai_ref, br_ref, bi_ref, cos_ref, nsin_ref):
    br = br_ref[...]
    bi = bi_ref[...]
    for r in range(TAB_TM // TAB_R):
        ar = ar_ref[r]
        ai = ai_ref[r]
        rows = slice(r * TAB_R, (r + 1) * TAB_R)
        cos_ref[rows, :] = (ar * br - ai * bi).astype(BF16)
        nsin_ref[rows, :] = (ai * br + ar * bi).astype(BF16)


def _latent_dft_tables():
    n = DEC_SEQ
    t = jnp.arange(n, dtype=jnp.int32)
    s1 = jnp.arange(n // TAB_R, dtype=jnp.int32)
    s0 = jnp.arange(TAB_R, dtype=jnp.int32)
    ang_a = ((s1[:, None] * t[None, :] * TAB_R) % n).astype(F32) * (2.0 * math.pi / n)
    ang_b = ((s0[:, None] * t[None, :]) % n).astype(F32) * (2.0 * math.pi / n)
    scale = 1.0 / math.sqrt(n)
    ar = jnp.cos(ang_a).reshape(n // TAB_R, 1, n)
    ai = (-jnp.sin(ang_a)).reshape(n // TAB_R, 1, n)
    br = jnp.cos(ang_b) * scale
    bi = jnp.sin(ang_b) * (-scale)
    per = TAB_TM // TAB_R
    coarse = pl.BlockSpec((per, 1, n), lambda i: (i, 0, 0))
    fine = pl.BlockSpec((TAB_R, n), lambda i: (0, 0))
    out = pl.BlockSpec((TAB_TM, n), lambda i: (i, 0))
    return pl.pallas_call(
        _dft_table_kernel,
        out_shape=(jax.ShapeDtypeStruct((n, n), BF16), jax.ShapeDtypeStruct((n, n), BF16)),
        grid=(n // TAB_TM,),
        in_specs=[coarse, coarse, fine, fine],
        out_specs=(out, out),
        compiler_params=_cparams(("parallel",)),
        name="latent_dft_tables",
    )(ar, ai, br, bi)


def _dft_tables(n, scale):
    idx = jnp.arange(n, dtype=jnp.int32)
    ang = ((idx[:, None] * idx[None, :]) % n).astype(F32) * (2.0 * math.pi / n)
    return (jnp.cos(ang) * scale).astype(BF16), (jnp.sin(ang) * (-scale)).astype(BF16)


def _rope_tables(tm):
    rows = DEC_SEQ // GRID_W
    row = jnp.repeat(jnp.arange(rows, dtype=F32), GRID_W)
    col = jnp.tile(jnp.arange(GRID_W, dtype=F32), rows)
    inv = 1.0 / (ROPE_THETA ** (jnp.arange(AXIS_FREQS, dtype=F32) / AXIS_FREQS))
    ar = row[:, None] * inv
    ac = col[:, None] * inv
    pad = jnp.zeros((DEC_SEQ, ROPE_PAD - QK_ROPE), F32)
    cos = jnp.concatenate([jnp.cos(ar), jnp.cos(ar), jnp.cos(ac), jnp.cos(ac), pad + 1.0], axis=1)
    sin = jnp.concatenate([-jnp.sin(ar), jnp.sin(ar), -jnp.sin(ac), jnp.sin(ac), pad], axis=1)
    cos = jnp.concatenate([cos, jnp.ones((tm, ROPE_PAD), F32)], axis=0)
    sin = jnp.concatenate([sin, jnp.zeros((tm, ROPE_PAD), F32)], axis=0)
    return cos, sin


def _pad_heads(w, part, width):
    k = w.shape[0]
    w = w.reshape(k, N_HEADS, -1)[:, :, part]
    return jnp.pad(w, ((0, 0), (0, 0), (0, width - w.shape[-1])))


def kernel(x_prompt, x_sample, cache_ckv, cache_kpe, c, c_ctx, w_mod, b_mod, norm_g, ffn_wg, ffn_wu,
           ffn_wd, mla_w_dq, mla_q_norm, mla_w_uq, mla_w_dkv, mla_kv_norm, mla_w_ukv, mla_w_o,
           fourier_w, fourier_b, final_norm):
    xs = (x_prompt.reshape(N_CTX, D_MODEL), x_sample.reshape(N_LAT, D_MODEL))
    cond = jnp.concatenate([c_ctx[None, :], c, jnp.zeros((N_GROUPS - 1 - DEC_BATCH, D_MODEL), F32)], axis=0)
    modt = _modulation(cond, w_mod, b_mod)

    wg = ffn_wg.astype(BF16)
    wu = ffn_wu.astype(BF16)
    wd = ffn_wd.astype(BF16)

    rope_cos, rope_sin = _rope_tables(MLA_TM)
    ch_cos, ch_nsin = _dft_tables(FGROUP_DIM, 1.0 / math.sqrt(FGROUP_DIM))
    ch_tab = jnp.concatenate([ch_cos, -ch_nsin], axis=1)
    ctx_cos, ctx_nsin = _dft_tables(SEQ, 1.0 / math.sqrt(SEQ))
    lat_cos, lat_nsin = _latent_dft_tables()

    states = None
    for l in range(DEPTH):
        j = l // 2
        x = _ffn(xs if l == 0 else (x,), modt, l, 0, norm_g[l, 0], wg[l, 0], wu[l, 0], wd[l, 0],
                 final_norm, False)
        if l % 2 == 0:
            wuq = mla_w_uq[j]
            wuq_p = jnp.concatenate(
                [_pad_heads(wuq, slice(0, QK_NOPE), QK_NOPE),
                 _pad_heads(wuq, slice(QK_NOPE, QK_NOPE + QK_ROPE), ROPE_PAD)], axis=-1)
            wuq_p = wuq_p.reshape(Q_LORA, N_HEADS * HEAD_PAD).astype(BF16)
            wdkv_p = jnp.pad(mla_w_dkv[j], ((0, 0), (0, ROPE_PAD - QK_ROPE))).astype(BF16)
            wukv = mla_w_ukv[j]
            wuk = _pad_heads(wukv, slice(0, QK_NOPE), QK_NOPE).reshape(KV_LORA, -1).astype(BF16)
            wuvt = _pad_heads(wukv, slice(QK_NOPE, QK_NOPE + V_DIM), V_DIM).reshape(KV_LORA, -1).T.astype(BF16)
            q, k, v, vt, new_ckv, new_kpe = _mla_proj(
                x, modt, l, norm_g[l, 1], mla_w_dq[j].astype(BF16), mla_q_norm[j], wuq_p, wdkv_p,
                mla_kv_norm[j], wuk, wuvt, rope_cos, rope_sin, states)
            states = (new_ckv, new_kpe)
            kc, vct = _cache_kv(
                cache_ckv[:, j].reshape(DEC_BATCH * PAST_LEN, KV_LORA),
                jnp.pad(cache_kpe[:, j].reshape(DEC_BATCH * PAST_LEN, QK_ROPE),
                        ((0, 0), (0, ROPE_PAD - QK_ROPE))),
                wuk, wuvt)
            o = _ctx_attn(q, k, v)
            o = _lat_attn(q, k, vt, kc, vct, o)
            x = _oproj(x, o, mla_w_o[j].astype(BF16), modt, l)
        else:
            a, b = _fourier_ch(x, modt, l, norm_g[l, 1], ch_tab)
            fw = fourier_w[j].astype(BF16)
            y = _fourier_ctx(a, b, ctx_cos, ctx_nsin, x, fw, fourier_b[j], modt, l)
            x = _fourier_lat(a, b, lat_cos, lat_nsin, x, fw, fourier_b[j], modt, l, y)
        x = _ffn((x,), modt, l, 6, norm_g[l, 2], wg[l, 1], wu[l, 1], wd[l, 1], final_norm,
                 l == DEPTH - 1)

    y_ctx, y_lat = x
    new_ckv, new_kpe = states
    return (y_ctx.reshape(BATCH, SEQ, D_MODEL), y_lat.reshape(DEC_BATCH, DEC_SEQ, D_MODEL),
            new_ckv, new_kpe)
```

```python
import functools
import math

import jax
import jax.numpy as jnp
from jax import lax
from jax.experimental import pallas as pl
from jax.experimental.pallas import tpu as pltpu

F32 = jnp.float32
BF16 = jnp.bfloat16

D_MODEL = 1024
BATCH = 32
SEQ = 256
DEPTH = 4
DEC_BATCH = 2
DEC_SEQ = 4096
PAST_LEN = 512
GRID_W = 64
N_HEADS = 8
QK_NOPE = 128
QK_ROPE = 64
V_DIM = 128
Q_LORA = 512
KV_LORA = 256
AXIS_FREQS = QK_ROPE // 4
ROPE_THETA = 10000.0
N_FGROUPS = 4
FGROUP_DIM = D_MODEL // N_FGROUPS
D_FF = 2816
N_MOD = 9
EPS = 1e-6
ATTN_SCALE = 1.0 / math.sqrt(QK_NOPE + QK_ROPE)
Q_SCALE = ATTN_SCALE * math.log2(math.e)

N_CTX = BATCH * SEQ
N_LAT = DEC_BATCH * DEC_SEQ
N_TOK = N_CTX + N_LAT
N_GROUPS = 8
HEAD_PAD = 256
ROPE_PAD = 128
VMEM_LIMIT = 56 * 1024 * 1024


def _cparams(sem):
    return pltpu.CompilerParams(dimension_semantics=sem, vmem_limit_bytes=VMEM_LIMIT)


def _group(i, tm):
    n_ctx = N_CTX // tm
    per_batch = DEC_SEQ // tm
    return jnp.where(i < n_ctx, 0, 1 + (i - n_ctx) // per_batch)


def _rms(x, g):
    return x * lax.rsqrt(jnp.mean(x * x, axis=-1, keepdims=True) + EPS) * g


def _mod_rms(x, g, mod, k):
    return _rms(x, g) * (1.0 + mod[k + 1:k + 2, :]) + mod[k:k + 1, :]


def _mod_kernel(c_ref, w_ref, b_ref, o_ref):
    c = c_ref[...]
    s = (c * jax.nn.sigmoid(c)).astype(BF16)
    o_ref[...] = jnp.dot(s, w_ref[...].astype(BF16), preferred_element_type=F32) + b_ref[...]


def _modulation(cond, w_mod, b_mod):
    out = pl.pallas_call(
        _mod_kernel,
        out_shape=jax.ShapeDtypeStruct((DEPTH, N_MOD, N_GROUPS, D_MODEL), F32),
        grid=(DEPTH, N_MOD),
        in_specs=[
            pl.BlockSpec((N_GROUPS, D_MODEL), lambda l, k: (0, 0)),
            pl.BlockSpec((None, D_MODEL, D_MODEL), lambda l, k: (l, 0, k)),
            pl.BlockSpec((None, None, 1, D_MODEL), lambda l, k: (l, k, 0, 0)),
        ],
        out_specs=pl.BlockSpec((None, None, N_GROUPS, D_MODEL), lambda l, k: (l, k, 0, 0)),
        compiler_params=_cparams(("arbitrary", "arbitrary")),
        name="adaln_modulation",
    )(cond, w_mod, b_mod.reshape(DEPTH, N_MOD, 1, D_MODEL))
    return out.transpose(0, 2, 1, 3)


FFN_TM = 512
FFN_TF = 256
FFN_CTX_TILES = N_CTX // FFN_TM


def _ffn_kernel(*refs, k0, split_in, final):
    n_x = 2 if split_in else 1
    x_refs, (mod_ref, g_ref, wg_ref, wu_ref, wd_ref, fn_ref) = refs[:n_x], refs[n_x:n_x + 6]
    n_out = 2 if final else 1
    out_refs, (a_ref,) = refs[n_x + 6:n_x + 6 + n_out], refs[n_x + 6 + n_out:]
    is_ctx = pl.program_id(0) < FFN_CTX_TILES

    if split_in:
        x = jnp.where(is_ctx, x_refs[0][...], x_refs[1][...])
    else:
        x = x_refs[0][...]
    h = _mod_rms(x, g_ref[...], mod_ref, k0).astype(BF16)
    for c in range(D_FF // FFN_TF):
        cols = slice(c * FFN_TF, (c + 1) * FFN_TF)
        g = jnp.dot(h, wg_ref[:, cols], preferred_element_type=F32)
        u = jnp.dot(h, wu_ref[:, cols], preferred_element_type=F32)
        a_ref[:, cols] = (g * jax.nn.sigmoid(g) * u).astype(BF16)
    y = jnp.dot(a_ref[...], wd_ref[...], preferred_element_type=F32)
    y = x + (0.5 * mod_ref[k0 + 2:k0 + 3, :]) * y
    if final:
        y = _rms(y, fn_ref[...])

        @pl.when(is_ctx)
        def _():
            out_refs[0][...] = y

        @pl.when(jnp.logical_not(is_ctx))
        def _():
            out_refs[1][...] = y
    else:
        out_refs[0][...] = y


def _ffn(xs, modt, l, half, g, wg, wu, wd, final_norm, final):
    tm = FFN_TM
    k0 = 6 * half
    split_in = len(xs) == 2
    rows = pl.BlockSpec((tm, D_MODEL), lambda i: (i, 0))
    ctx_rows = pl.BlockSpec((tm, D_MODEL), lambda i: (jnp.minimum(i, FFN_CTX_TILES - 1), 0))
    lat_rows = pl.BlockSpec((tm, D_MODEL), lambda i: (jnp.maximum(i - FFN_CTX_TILES, 0), 0))
    resident = lambda shape: pl.BlockSpec((None, None) + shape, lambda i: (l, half, 0, 0),
                                          pipeline_mode=pl.Buffered(1))
    if final:
        out_shape = (jax.ShapeDtypeStruct((N_CTX, D_MODEL), F32),
                     jax.ShapeDtypeStruct((N_LAT, D_MODEL), F32))
        out_specs = (ctx_rows, lat_rows)
    else:
        out_shape = jax.ShapeDtypeStruct((N_TOK, D_MODEL), F32)
        out_specs = rows
    return pl.pallas_call(
        functools.partial(_ffn_kernel, k0=k0, split_in=split_in, final=final),
        out_shape=out_shape,
        grid=(N_TOK // tm,),
        in_specs=([ctx_rows, lat_rows] if split_in else [rows]) + [
            pl.BlockSpec((None, None, N_MOD, D_MODEL), lambda i: (l, _group(i, tm), 0, 0)),
            pl.BlockSpec((1, D_MODEL), lambda i: (0, 0)),
            resident((D_MODEL, D_FF)),
            resident((D_MODEL, D_FF)),
            resident((D_FF, D_MODEL)),
            pl.BlockSpec((1, D_MODEL), lambda i: (0, 0)),
        ],
        out_specs=out_specs,
        scratch_shapes=[pltpu.VMEM((tm, D_FF), BF16)],
        compiler_params=_cparams(("arbitrary",)),
        name="swiglu_halfstep",
    )(*xs, modt, g.reshape(1, D_MODEL), wg, wu, wd, final_norm.reshape(1, D_MODEL))


MLA_TM = 512


def _swap_halves(x):
    lane = lax.broadcasted_iota(jnp.int32, x.shape, 1)
    fwd = pltpu.roll(x, ROPE_PAD - AXIS_FREQS, axis=1)
    bwd = pltpu.roll(x, AXIS_FREQS, axis=1)
    return jnp.where(lane % (2 * AXIS_FREQS) < AXIS_FREQS, fwd, bwd)


def _rope(x, cos, sin):
    return x * cos + _swap_halves(x) * sin


def _kv_up(ckv, kpe_pad, wuk_ref, wuvt_ref, k_ref, vt_ref):
    c = ckv.astype(BF16)
    kn = jnp.dot(c, wuk_ref[...], preferred_element_type=F32).astype(BF16)
    vt = lax.dot_general(wuvt_ref[...], c, _NT, preferred_element_type=F32)
    vt_ref[...] = vt.astype(BF16)
    kp = kpe_pad.astype(BF16)
    for hd in range(N_HEADS):
        k_ref[:, hd * HEAD_PAD:hd * HEAD_PAD + QK_NOPE] = kn[:, hd * QK_NOPE:(hd + 1) * QK_NOPE]
        k_ref[:, hd * HEAD_PAD + QK_NOPE:(hd + 1) * HEAD_PAD] = kp


def _mla_proj_kernel(*refs, has_prev):
    (x_ref, mod_ref, g_ref, wdq_ref, qn_ref, wuq_ref, wdkv_ref, kvn_ref, wuk_ref, wuvt_ref,
     cos_ref, sin_ref) = refs[:12]
    q_ref, k_ref, vt_ref, ckv_ref, kpe_ref = refs[12 + (2 if has_prev else 0):]
    h = _mod_rms(x_ref[...], g_ref[...], mod_ref, 3).astype(BF16)
    cq = _rms(jnp.dot(h, wdq_ref[...], preferred_element_type=F32), qn_ref[...]).astype(BF16)
    q = jnp.dot(cq, wuq_ref[...], preferred_element_type=F32)
    cos = cos_ref[...]
    sin = sin_ref[...]
    for hd in range(N_HEADS):
        lo = hd * HEAD_PAD
        q_ref[:, lo:lo + QK_NOPE] = (q[:, lo:lo + QK_NOPE] * Q_SCALE).astype(BF16)
        qr = _rope(q[:, lo + QK_NOPE:lo + HEAD_PAD], cos, sin)
        q_ref[:, lo + QK_NOPE:lo + HEAD_PAD] = (qr * Q_SCALE).astype(BF16)
    kv = jnp.dot(h, wdkv_ref[...], preferred_element_type=F32)
    ckv = _rms(kv[:, :KV_LORA], kvn_ref[...])
    kpe = _rope(kv[:, KV_LORA:], cos, sin)

    @pl.when(pl.program_id(0) < N_CTX // MLA_TM)
    def _():
        ckv_ref[...] = ckv.reshape(ckv_ref.shape)
        kpe_ref[...] = kpe[:, :QK_ROPE].reshape(kpe_ref.shape)

    _kv_up(ckv, kpe, wuk_ref, wuvt_ref, k_ref, vt_ref)


def _mla_proj(x, modt, l, g, wdq, qn, wuq, wdkv, kvn, wuk, wuvt, cos_t, sin_t, prev_states):
    tm = MLA_TM
    n_ctx = N_CTX // tm
    per_batch = DEC_SEQ // tm
    slot = l // 2
    n_mla = (DEPTH + 1) // 2

    def rope_idx(i):
        return jnp.where(i < n_ctx, per_batch, (i - n_ctx) % per_batch)

    full = lambda shape: pl.BlockSpec(shape, lambda i: (0, 0))
    rows = lambda w: pl.BlockSpec((tm, w), lambda i: (i, 0))
    state = lambda w: pl.BlockSpec((tm // SEQ, None, SEQ, w),
                                   lambda i: (jnp.minimum(i, n_ctx - 1), slot, 0, 0))
    has_prev = prev_states is not None
    prev_specs = [pl.BlockSpec(memory_space=pl.ANY)] * 2 if has_prev else []
    return pl.pallas_call(
        functools.partial(_mla_proj_kernel, has_prev=has_prev),
        out_shape=(
            jax.ShapeDtypeStruct((N_TOK, N_HEADS * HEAD_PAD), BF16),
            jax.ShapeDtypeStruct((N_TOK, N_HEADS * HEAD_PAD), BF16),
            jax.ShapeDtypeStruct((N_HEADS * V_DIM, N_TOK), BF16),
            jax.ShapeDtypeStruct((BATCH, n_mla, SEQ, KV_LORA), F32),
            jax.ShapeDtypeStruct((BATCH, n_mla, SEQ, QK_ROPE), F32),
        ),
        grid=(N_TOK // tm,),
        in_specs=[
            rows(D_MODEL),
            pl.BlockSpec((None, None, N_MOD, D_MODEL), lambda i: (l, _group(i, tm), 0, 0)),
            full((1, D_MODEL)),
            full((D_MODEL, Q_LORA)),
            full((1, Q_LORA)),
            full((Q_LORA, N_HEADS * HEAD_PAD)),
            full((D_MODEL, KV_LORA + ROPE_PAD)),
            full((1, KV_LORA)),
            full((KV_LORA, N_HEADS * QK_NOPE)),
            full((N_HEADS * V_DIM, KV_LORA)),
            pl.BlockSpec((tm, ROPE_PAD), lambda i: (rope_idx(i), 0)),
            pl.BlockSpec((tm, ROPE_PAD), lambda i: (rope_idx(i), 0)),
        ] + prev_specs,
        out_specs=(rows(N_HEADS * HEAD_PAD), rows(N_HEADS * HEAD_PAD),
                   pl.BlockSpec((N_HEADS * V_DIM, tm), lambda i: (0, i)),
                   state(KV_LORA), state(QK_ROPE)),
        input_output_aliases={12: 3, 13: 4} if has_prev else {},
        compiler_params=_cparams(("arbitrary",)),
        name="mla_projections",
    )(x, modt, g.reshape(1, D_MODEL), wdq, qn.reshape(1, Q_LORA), wuq, wdkv,
      kvn.reshape(1, KV_LORA), wuk, wuvt, cos_t, sin_t, *(prev_states or ()))


def _cache_kv_kernel(ckv_ref, kpe_ref, wuk_ref, wuvt_ref, k_ref, vt_ref):
    _kv_up(ckv_ref[...], kpe_ref[...], wuk_ref, wuvt_ref, k_ref, vt_ref)


def _cache_kv(ckv, kpe_pad, wuk, wuvt):
    n = DEC_BATCH * PAST_LEN
    tm = PAST_LEN
    full = lambda shape: pl.BlockSpec(shape, lambda i: (0, 0))
    rows = lambda w: pl.BlockSpec((tm, w), lambda i: (i, 0))
    return pl.pallas_call(
        _cache_kv_kernel,
        out_shape=(jax.ShapeDtypeStruct((n, N_HEADS * HEAD_PAD), BF16),
                   jax.ShapeDtypeStruct((N_HEADS * V_DIM, n), BF16)),
        grid=(n // tm,),
        in_specs=[rows(KV_LORA), rows(ROPE_PAD),
                  full((KV_LORA, N_HEADS * QK_NOPE)), full((N_HEADS * V_DIM, KV_LORA))],
        out_specs=(rows(N_HEADS * HEAD_PAD), pl.BlockSpec((N_HEADS * V_DIM, tm), lambda i: (0, i))),
        compiler_params=_cparams(("parallel",)),
        name="cache_kv_up",
    )(ckv, kpe_pad, wuk, wuvt)


_NT = (((1,), (1,)), ((), ()))


def _ctx_attn_kernel(q_ref, k_ref, vt_ref, o_ref):
    for hd in range(N_HEADS):
        q = q_ref[:, hd * HEAD_PAD:(hd + 1) * HEAD_PAD]
        k = k_ref[:, hd * HEAD_PAD:(hd + 1) * HEAD_PAD]
        st = lax.dot_general(k, q, _NT, preferred_element_type=F32)
        p = jnp.exp2(st - jnp.max(st, axis=0, keepdims=True))
        ot = jnp.dot(vt_ref[hd * V_DIM:(hd + 1) * V_DIM, :], p.astype(BF16),
                     preferred_element_type=F32)
        ot = ot / jnp.sum(p, axis=0, keepdims=True)
        o_ref[:, hd * V_DIM:(hd + 1) * V_DIM] = ot.T.astype(BF16)


def _ctx_attn(q, k, vt):
    rows = lambda w: pl.BlockSpec((SEQ, w), lambda b: (b, 0))
    return pl.pallas_call(
        _ctx_attn_kernel,
        out_shape=jax.ShapeDtypeStruct((N_TOK, N_HEADS * V_DIM), BF16),
        grid=(BATCH,),
        in_specs=[rows(N_HEADS * HEAD_PAD), rows(N_HEADS * HEAD_PAD),
                  pl.BlockSpec((N_HEADS * V_DIM, SEQ), lambda b: (0, b))],
        out_specs=rows(N_HEADS * V_DIM),
        compiler_params=_cparams(("parallel",)),
        name="context_attention",
    )(q, k, vt)


LAT_TQ = 512
LAT_KC = 512
LAT_KEYS = PAST_LEN + DEC_SEQ
SUBLANES = 8


def _lat_attn_kernel(q_ref, kc_ref, kl_ref, vct_ref, vlt_ref, o_prev_ref, o_ref, s_ref, m_ref):
    del o_prev_ref
    tq = q_ref.shape[0]
    kc = LAT_KC
    n_cache = PAST_LEN // kc
    n_chunks = LAT_KEYS // kc
    step = pl.program_id(2)

    @pl.when((pl.program_id(0) == 0) & (pl.program_id(1) == 0) & (step == 0))
    def _():
        s_ref[1] = jnp.zeros(s_ref.shape[1:], F32)
        m_ref[1] = jnp.zeros(m_ref.shape[1:], F32)

    def rows8(x, op):
        return op(x.reshape(kc // SUBLANES, SUBLANES, tq), axis=0)

    def body(cur):
        prev = 1 - cur
        q = q_ref[...]
        m = jnp.max(m_ref[prev], axis=0, keepdims=True)
        mpart = None
        lpart = jnp.zeros((SUBLANES, tq), F32)
        acc = jnp.zeros((V_DIM, tq), F32)
        for c in range(n_chunks):
            rows = slice(c * kc, (c + 1) * kc)
            if c < n_cache:
                k = kc_ref[rows, :]
                vt = vct_ref[:, rows]
            else:
                lat = slice((c - n_cache) * kc, (c - n_cache + 1) * kc)
                k = kl_ref[lat, :]
                vt = vlt_ref[:, lat]
            st = lax.dot_general(k, q, _NT, preferred_element_type=F32)
            s_ref[cur, rows, :] = st
            cm = rows8(st, jnp.max)
            mpart = cm if mpart is None else jnp.maximum(mpart, cm)
            p = jnp.exp2(s_ref[prev, rows, :] - m)
            lpart = lpart + rows8(p, jnp.sum)
            acc = acc + jnp.dot(vt, p.astype(BF16), preferred_element_type=F32)
        m_ref[cur] = mpart
        denom = jnp.sum(lpart, axis=0, keepdims=True)
        o_ref[...] = (acc / denom).T.astype(BF16)

    for parity in range(2):
        pl.when(step % 2 == parity)(functools.partial(body, parity))


def _lat_attn(q, k, vt, kc, vct, o_ctx):
    tq = LAT_TQ
    q_off = N_CTX // tq
    l_off = N_CTX // DEC_SEQ
    nq = DEC_SEQ // tq
    return pl.pallas_call(
        _lat_attn_kernel,
        out_shape=jax.ShapeDtypeStruct((N_TOK, N_HEADS * V_DIM), BF16),
        grid=(DEC_BATCH, N_HEADS, nq + 1),
        in_specs=[
            pl.BlockSpec((tq, HEAD_PAD),
                         lambda b, h, i: (q_off + b * nq + jnp.minimum(i, nq - 1), h)),
            pl.BlockSpec((PAST_LEN, HEAD_PAD), lambda b, h, i: (b, h)),
            pl.BlockSpec((DEC_SEQ, HEAD_PAD), lambda b, h, i: (l_off + b, h)),
            pl.BlockSpec((V_DIM, PAST_LEN), lambda b, h, i: (h, b)),
            pl.BlockSpec((V_DIM, DEC_SEQ), lambda b, h, i: (h, l_off + b)),
            pl.BlockSpec(memory_space=pl.ANY),
        ],
        out_specs=pl.BlockSpec((tq, V_DIM),
                               lambda b, h, i: (q_off + b * nq + jnp.maximum(i - 1, 0), h)),
        scratch_shapes=[pltpu.VMEM((2, LAT_KEYS, tq), F32), pltpu.VMEM((2, SUBLANES, tq), F32)],
        input_output_aliases={5: 0},
        compiler_params=_cparams(("arbitrary", "arbitrary", "arbitrary")),
        name="latent_attention",
    )(q, kc, k, vct, vt, o_ctx)


OPROJ_TM = 1024


def _oproj_kernel(x_ref, o_ref, w_ref, mod_ref, y_ref):
    y_ref[...] = x_ref[...] + mod_ref[5:6, :] * jnp.dot(o_ref[...], w_ref[...],
                                                       preferred_element_type=F32)


def _oproj(x, o, w, modt, l):
    tm = OPROJ_TM
    return pl.pallas_call(
        _oproj_kernel,
        out_shape=jax.ShapeDtypeStruct((N_TOK, D_MODEL), F32),
        grid=(N_TOK // tm,),
        in_specs=[
            pl.BlockSpec((tm, D_MODEL), lambda i: (i, 0)),
            pl.BlockSpec((tm, N_HEADS * V_DIM), lambda i: (i, 0)),
            pl.BlockSpec((N_HEADS * V_DIM, D_MODEL), lambda i: (0, 0)),
            pl.BlockSpec((None, None, N_MOD, D_MODEL), lambda i: (l, _group(i, tm), 0, 0)),
        ],
        out_specs=pl.BlockSpec((tm, D_MODEL), lambda i: (i, 0)),
        compiler_params=_cparams(("parallel",)),
        name="attn_out_proj",
    )(x, o, w, modt)


FCH_TM = 1024


def _fourier_ch_kernel(x_ref, mod_ref, g_ref, cs_ref, a_ref, b_ref):
    h = _mod_rms(x_ref[...], g_ref[...], mod_ref, 3).astype(BF16)
    for gi in range(N_FGROUPS):
        lo = gi * FGROUP_DIM
        r = jnp.dot(h[:, lo:lo + FGROUP_DIM], cs_ref[...], preferred_element_type=F32)
        a_ref[:, lo:lo + FGROUP_DIM] = r[:, :FGROUP_DIM].astype(BF16)
        b_ref[:, lo:lo + FGROUP_DIM] = r[:, FGROUP_DIM:].astype(BF16)


def _fourier_ch(x, modt, l, g, ch_tab):
    tm = FCH_TM
    rows = pl.BlockSpec((tm, D_MODEL), lambda i: (i, 0))
    return pl.pallas_call(
        _fourier_ch_kernel,
        out_shape=(jax.ShapeDtypeStruct((N_TOK, D_MODEL), BF16),
                   jax.ShapeDtypeStruct((N_TOK, D_MODEL), BF16)),
        grid=(N_TOK // tm,),
        in_specs=[
            rows,
            pl.BlockSpec((None, None, N_MOD, D_MODEL), lambda i: (l, _group(i, tm), 0, 0)),
            pl.BlockSpec((1, D_MODEL), lambda i: (0, 0)),
            pl.BlockSpec((FGROUP_DIM, 2 * FGROUP_DIM), lambda i: (0, 0)),
        ],
        out_specs=(rows, rows),
        compiler_params=_cparams(("parallel",)),
        name="fourier_channel_dft",
    )(x, modt, g.reshape(1, D_MODEL), ch_tab)


def _fourier_epilogue(f, x_ref, w_ref, bias_ref, mod_ref):
    mixed = jnp.dot(f.astype(BF16), w_ref[...], preferred_element_type=F32) + bias_ref[...]
    return x_ref[...] + mod_ref[5:6, :] * mixed


def _fourier_ctx_kernel(a_ref, b_ref, c_ref, s_ref, x_ref, w_ref, bias_ref, mod_ref, y_ref):
    f = jnp.dot(c_ref[...], a_ref[...], preferred_element_type=F32)
    f = f + jnp.dot(s_ref[...], b_ref[...], preferred_element_type=F32)
    y_ref[...] = _fourier_epilogue(f, x_ref, w_ref, bias_ref, mod_ref)


def _fourier_ctx(a, b, cos_t, nsin_t, x, w, bias, modt, l):
    rows = lambda dt: pl.BlockSpec((SEQ, D_MODEL), lambda i: (i, 0))
    full = lambda shape: pl.BlockSpec(shape, lambda i: (0, 0))
    return pl.pallas_call(
        _fourier_ctx_kernel,
        out_shape=jax.ShapeDtypeStruct((N_TOK, D_MODEL), F32),
        grid=(BATCH,),
        in_specs=[rows(BF16), rows(BF16), full((SEQ, SEQ)), full((SEQ, SEQ)), rows(F32),
                  full((D_MODEL, D_MODEL)), full((1, D_MODEL)),
                  pl.BlockSpec((None, None, N_MOD, D_MODEL), lambda i: (l, 0, 0, 0))],
        out_specs=rows(F32),
        compiler_params=_cparams(("parallel",)),
        name="fourier_context_positions",
    )(a, b, cos_t, nsin_t, x, w, bias.reshape(1, D_MODEL), modt)


FLAT_TM = 1024
FLAT_TK = 1024


def _fourier_lat_kernel(a_ref, b_ref, c_ref, s_ref, x_ref, w_ref, bias_ref, mod_ref, y_prev_ref,
                        y_ref, acc_ref):
    del y_prev_ref
    k = pl.program_id(2)

    @pl.when(k == 0)
    def _():
        acc_ref[...] = jnp.zeros_like(acc_ref)

    part = jnp.dot(c_ref[...], a_ref[...], preferred_element_type=F32)
    part = part + jnp.dot(s_ref[...], b_ref[...], preferred_element_type=F32)
    acc_ref[...] += part

    @pl.when(k == pl.num_programs(2) - 1)
    def _():
        y_ref[...] = _fourier_epilogue(acc_ref[...], x_ref, w_ref, bias_ref, mod_ref)


def _fourier_lat(a, b, cos_t, nsin_t, x, w, bias, modt, l, y_ctx):
    tm, tk = FLAT_TM, FLAT_TK
    ni = DEC_SEQ // tm
    nk = DEC_SEQ // tk
    r_off = N_CTX // tm
    k_off = N_CTX // tk
    src = pl.BlockSpec((tk, D_MODEL), lambda bb, i, k: (k_off + bb * nk + k, 0))
    tab = pl.BlockSpec((tm, tk), lambda bb, i, k: (i, k))
    xrow = pl.BlockSpec((tm, D_MODEL), lambda bb, i, k: (r_off + bb * ni + i, 0))
    return pl.pallas_call(
        _fourier_lat_kernel,
        out_shape=jax.ShapeDtypeStruct((N_TOK, D_MODEL), F32),
        grid=(DEC_BATCH, ni, nk),
        in_specs=[src, src, tab, tab, xrow,
                  pl.BlockSpec((D_MODEL, D_MODEL), lambda bb, i, k: (0, 0)),
                  pl.BlockSpec((1, D_MODEL), lambda bb, i, k: (0, 0)),
                  pl.BlockSpec((None, None, N_MOD, D_MODEL), lambda bb, i, k: (l, 1 + bb, 0, 0)),
                  pl.BlockSpec(memory_space=pl.ANY)],
        out_specs=xrow,
        scratch_shapes=[pltpu.VMEM((tm, D_MODEL), F32)],
        input_output_aliases={8: 0},
        compiler_params=_cparams(("parallel", "parallel", "arbitrary")),
        name="fourier_latent_positions",
    )(a, b, cos_t, nsin_t, x, w, bias.reshape(1, D_MODEL), modt, y_ctx)


TAB_R = 64
TAB_TM = 256


def _dft_table_kernel(ar_ref, ai_ref, br_ref, bi_ref, cos_ref, nsin_ref):
    br = br_ref[...]
    bi = bi_ref[...]
    for r in range(TAB_TM // TAB_R):
        ar = ar_ref[r]
        ai = ai_ref[r]
        rows = slice(r * TAB_R, (r + 1) * TAB_R)
        cos_ref[rows, :] = (ar * br - ai * bi).astype(BF16)
        nsin_ref[rows, :] = (ai * br + ar * bi).astype(BF16)


def _latent_dft_tables():
    n = DEC_SEQ
    t = jnp.arange(n, dtype=jnp.int32)
    s1 = jnp.arange(n // TAB_R, dtype=jnp.int32)
    s0 = jnp.arange(TAB_R, dtype=jnp.int32)
    ang_a = ((s1[:, None] * t[None, :] * TAB_R) % n).astype(F32) * (2.0 * math.pi / n)
    ang_b = ((s0[:, None] * t[None, :]) % n).astype(F32) * (2.0 * math.pi / n)
    scale = 1.0 / math.sqrt(n)
    ar = jnp.cos(ang_a).reshape(n // TAB_R, 1, n)
    ai = (-jnp.sin(ang_a)).reshape(n // TAB_R, 1, n)
    br = jnp.cos(ang_b) * scale
    bi = jnp.sin(ang_b) * (-scale)
    per = TAB_TM // TAB_R
    coarse = pl.BlockSpec((per, 1, n), lambda i: (i, 0, 0))
    fine = pl.BlockSpec((TAB_R, n), lambda i: (0, 0))
    out = pl.BlockSpec((TAB_TM, n), lambda i: (i, 0))
    return pl.pallas_call(
        _dft_table_kernel,
        out_shape=(jax.ShapeDtypeStruct((n, n), BF16), jax.ShapeDtypeStruct((n, n), BF16)),
        grid=(n // TAB_TM,),
        in_specs=[coarse, coarse, fine, fine],
        out_specs=(out, out),
        compiler_params=_cparams(("parallel",)),
        name="latent_dft_tables",
    )(ar, ai, br, bi)


def _dft_tables(n, scale):
    idx = jnp.arange(n, dtype=jnp.int32)
    ang = ((idx[:, None] * idx[None, :]) % n).astype(F32) * (2.0 * math.pi / n)
    return (jnp.cos(ang) * scale).astype(BF16), (jnp.sin(ang) * (-scale)).astype(BF16)


def _rope_tables(tm):
    rows = DEC_SEQ // GRID_W
    row = jnp.repeat(jnp.arange(rows, dtype=F32), GRID_W)
    col = jnp.tile(jnp.arange(GRID_W, dtype=F32), rows)
    inv = 1.0 / (ROPE_THETA ** (jnp.arange(AXIS_FREQS, dtype=F32) / AXIS_FREQS))
    ar = row[:, None] * inv
    ac = col[:, None] * inv
    pad = jnp.zeros((DEC_SEQ, ROPE_PAD - QK_ROPE), F32)
    cos = jnp.concatenate([jnp.cos(ar), jnp.cos(ar), jnp.cos(ac), jnp.cos(ac), pad + 1.0], axis=1)
    sin = jnp.concatenate([-jnp.sin(ar), jnp.sin(ar), -jnp.sin(ac), jnp.sin(ac), pad], axis=1)
    cos = jnp.concatenate([cos, jnp.ones((tm, ROPE_PAD), F32)], axis=0)
    sin = jnp.concatenate([sin, jnp.zeros((tm, ROPE_PAD), F32)], axis=0)
    return cos, sin


def _pad_heads(w, part, width):
    k = w.shape[0]
    w = w.reshape(k, N_HEADS, -1)[:, :, part]
    return jnp.pad(w, ((0, 0), (0, 0), (0, width - w.shape[-1])))


def kernel(x_prompt, x_sample, cache_ckv, cache_kpe, c, c_ctx, w_mod, b_mod, norm_g, ffn_wg, ffn_wu,
           ffn_wd, mla_w_dq, mla_q_norm, mla_w_uq, mla_w_dkv, mla_kv_norm, mla_w_ukv, mla_w_o,
           fourier_w, fourier_b, final_norm):
    xs = (x_prompt.reshape(N_CTX, D_MODEL), x_sample.reshape(N_LAT, D_MODEL))
    cond = jnp.concatenate([c_ctx[None, :], c, jnp.zeros((N_GROUPS - 1 - DEC_BATCH, D_MODEL), F32)], axis=0)
    modt = _modulation(cond, w_mod, b_mod)

    wg = ffn_wg.astype(BF16)
    wu = ffn_wu.astype(BF16)
    wd = ffn_wd.astype(BF16)

    rope_cos, rope_sin = _rope_tables(MLA_TM)
    ch_cos, ch_nsin = _dft_tables(FGROUP_DIM, 1.0 / math.sqrt(FGROUP_DIM))
    ch_tab = jnp.concatenate([ch_cos, -ch_nsin], axis=1)
    ctx_cos, ctx_nsin = _dft_tables(SEQ, 1.0 / math.sqrt(SEQ))
    lat_cos, lat_nsin = _latent_dft_tables()

    states = None
    for l in range(DEPTH):
        j = l // 2
        x = _ffn(xs if l == 0 else (x,), modt, l, 0, norm_g[l, 0], wg, wu, wd, final_norm, False)
        if l % 2 == 0:
            wuq = mla_w_uq[j]
            wuq_p = jnp.concatenate(
                [_pad_heads(wuq, slice(0, QK_NOPE), QK_NOPE),
                 _pad_heads(wuq, slice(QK_NOPE, QK_NOPE + QK_ROPE), ROPE_PAD)], axis=-1)
            wuq_p = wuq_p.reshape(Q_LORA, N_HEADS * HEAD_PAD).astype(BF16)
            wdkv_p = jnp.pad(mla_w_dkv[j], ((0, 0), (0, ROPE_PAD - QK_ROPE))).astype(BF16)
            wukv = mla_w_ukv[j]
            wuk = _pad_heads(wukv, slice(0, QK_NOPE), QK_NOPE).reshape(KV_LORA, -1).astype(BF16)
            wuvt = _pad_heads(wukv, slice(QK_NOPE, QK_NOPE + V_DIM), V_DIM).reshape(KV_LORA, -1).T.astype(BF16)
            q, k, vt, new_ckv, new_kpe = _mla_proj(
                x, modt, l, norm_g[l, 1], mla_w_dq[j].astype(BF16), mla_q_norm[j], wuq_p, wdkv_p,
                mla_kv_norm[j], wuk, wuvt, rope_cos, rope_sin, states)
            states = (new_ckv, new_kpe)
            kc, vct = _cache_kv(
                cache_ckv[:, j].reshape(DEC_BATCH * PAST_LEN, KV_LORA),
                jnp.pad(cache_kpe[:, j].reshape(DEC_BATCH * PAST_LEN, QK_ROPE),
                        ((0, 0), (0, ROPE_PAD - QK_ROPE))),
                wuk, wuvt)
            o = _ctx_attn(q, k, vt)
            o = _lat_attn(q, k, vt, kc, vct, o)
            x = _oproj(x, o, mla_w_o[j].astype(BF16), modt, l)
        else:
            a, b = _fourier_ch(x, modt, l, norm_g[l, 1], ch_tab)
            fw = fourier_w[j].astype(BF16)
            y = _fourier_ctx(a, b, ctx_cos, ctx_nsin, x, fw, fourier_b[j], modt, l)
            x = _fourier_lat(a, b, lat_cos, lat_nsin, x, fw, fourier_b[j], modt, l, y)
        x = _ffn((x,), modt, l, 1, norm_g[l, 2], wg, wu, wd, final_norm, l == DEPTH - 1)

    y_ctx, y_lat = x
    new_ckv, new_kpe = states
    return (y_ctx.reshape(BATCH, SEQ, D_MODEL), y_lat.reshape(DEC_BATCH, DEC_SEQ, D_MODEL),
            new_ckv, new_kpe)
```

```python
import functools
import math

import jax
import jax.numpy as jnp
from jax import lax
from jax.experimental import pallas as pl
from jax.experimental.pallas import tpu as pltpu

F32 = jnp.float32
BF16 = jnp.bfloat16

D_MODEL = 1024
BATCH = 32
SEQ = 256
DEPTH = 4
DEC_BATCH = 2
DEC_SEQ = 4096
PAST_LEN = 512
GRID_W = 64
N_HEADS = 8
QK_NOPE = 128
QK_ROPE = 64
V_DIM = 128
Q_LORA = 512
KV_LORA = 256
AXIS_FREQS = QK_ROPE // 4
ROPE_THETA = 10000.0
N_FGROUPS = 4
FGROUP_DIM = D_MODEL // N_FGROUPS
D_FF = 2816
N_MOD = 9
EPS = 1e-6
ATTN_SCALE = 1.0 / math.sqrt(QK_NOPE + QK_ROPE)
Q_SCALE = ATTN_SCALE * math.log2(math.e)

N_CTX = BATCH * SEQ
N_LAT = DEC_BATCH * DEC_SEQ
N_TOK = N_CTX + N_LAT
N_GROUPS = 8
HEAD_PAD = 256
ROPE_PAD = 128
VMEM_LIMIT = 56 * 1024 * 1024


def _cparams(sem):
    return pltpu.CompilerParams(dimension_semantics=sem, vmem_limit_bytes=VMEM_LIMIT)


def _group(i, tm):
    n_ctx = N_CTX // tm
    per_batch = DEC_SEQ // tm
    return jnp.where(i < n_ctx, 0, 1 + (i - n_ctx) // per_batch)


def _rms(x, g):
    return x * lax.rsqrt(jnp.mean(x * x, axis=-1, keepdims=True) + EPS) * g


def _mod_rms(x, g, mod, k):
    gain = g * (1.0 + mod[k + 1:k + 2, :])
    return x * lax.rsqrt(jnp.mean(x * x, axis=-1, keepdims=True) + EPS) * gain + mod[k:k + 1, :]


def _mod_kernel(c_ref, w_ref, b_ref, o_ref):
    c = c_ref[...]
    s = (c * jax.nn.sigmoid(c)).astype(BF16)
    o_ref[...] = jnp.dot(s, w_ref[...].astype(BF16), preferred_element_type=F32) + b_ref[...]


def _modulation(cond, w_mod, b_mod):
    out = pl.pallas_call(
        _mod_kernel,
        out_shape=jax.ShapeDtypeStruct((DEPTH, N_MOD, N_GROUPS, D_MODEL), F32),
        grid=(DEPTH, N_MOD),
        in_specs=[
            pl.BlockSpec((N_GROUPS, D_MODEL), lambda l, k: (0, 0)),
            pl.BlockSpec((None, D_MODEL, D_MODEL), lambda l, k: (l, 0, k)),
            pl.BlockSpec((None, None, 1, D_MODEL), lambda l, k: (l, k, 0, 0)),
        ],
        out_specs=pl.BlockSpec((None, None, N_GROUPS, D_MODEL), lambda l, k: (l, k, 0, 0)),
        compiler_params=_cparams(("arbitrary", "arbitrary")),
        name="adaln_modulation",
    )(cond, w_mod, b_mod.reshape(DEPTH, N_MOD, 1, D_MODEL))
    return out.transpose(0, 2, 1, 3)


FFN_TM = 512
FFN_TF = 256
FFN_CTX_TILES = N_CTX // FFN_TM


def _ffn_kernel(*refs, k0, split_in, final):
    n_x = 2 if split_in else 1
    x_refs, (mod_ref, g_ref, wg_ref, wu_ref, wd_ref, fn_ref) = refs[:n_x], refs[n_x:n_x + 6]
    n_out = 2 if final else 1
    out_refs, (a_ref,) = refs[n_x + 6:n_x + 6 + n_out], refs[n_x + 6 + n_out:]
    is_ctx = pl.program_id(0) < FFN_CTX_TILES

    if split_in:
        x = jnp.where(is_ctx, x_refs[0][...], x_refs[1][...])
    else:
        x = x_refs[0][...]
    h = _mod_rms(x, g_ref[...], mod_ref, k0).astype(BF16)
    for c in range(D_FF // FFN_TF):
        cols = slice(c * FFN_TF, (c + 1) * FFN_TF)
        g = jnp.dot(h, wg_ref[:, cols], preferred_element_type=F32)
        u = jnp.dot(h, wu_ref[:, cols], preferred_element_type=F32)
        a_ref[:, cols] = (g * jax.nn.sigmoid(g) * u).astype(BF16)
    y = jnp.dot(a_ref[...], wd_ref[...], preferred_element_type=F32)
    y = x + (0.5 * mod_ref[k0 + 2:k0 + 3, :]) * y
    if final:
        y = _rms(y, fn_ref[...])

        @pl.when(is_ctx)
        def _():
            out_refs[0][...] = y

        @pl.when(jnp.logical_not(is_ctx))
        def _():
            out_refs[1][...] = y
    else:
        out_refs[0][...] = y


def _ffn(xs, modt, l, half, g, wg, wu, wd, final_norm, final):
    tm = FFN_TM
    k0 = 6 * half
    split_in = len(xs) == 2
    rows = pl.BlockSpec((tm, D_MODEL), lambda i: (i, 0))
    ctx_rows = pl.BlockSpec((tm, D_MODEL), lambda i: (jnp.minimum(i, FFN_CTX_TILES - 1), 0))
    lat_rows = pl.BlockSpec((tm, D_MODEL), lambda i: (jnp.maximum(i - FFN_CTX_TILES, 0), 0))
    resident = lambda shape: pl.BlockSpec((None, None) + shape, lambda i: (l, half, 0, 0),
                                          pipeline_mode=pl.Buffered(1))
    if final:
        out_shape = (jax.ShapeDtypeStruct((N_CTX, D_MODEL), F32),
                     jax.ShapeDtypeStruct((N_LAT, D_MODEL), F32))
        out_specs = (ctx_rows, lat_rows)
    else:
        out_shape = jax.ShapeDtypeStruct((N_TOK, D_MODEL), F32)
        out_specs = rows
    return pl.pallas_call(
        functools.partial(_ffn_kernel, k0=k0, split_in=split_in, final=final),
        out_shape=out_shape,
        grid=(N_TOK // tm,),
        in_specs=([ctx_rows, lat_rows] if split_in else [rows]) + [
            pl.BlockSpec((None, None, N_MOD, D_MODEL), lambda i: (l, _group(i, tm), 0, 0)),
            pl.BlockSpec((1, D_MODEL), lambda i: (0, 0)),
            resident((D_MODEL, D_FF)),
            resident((D_MODEL, D_FF)),
            resident((D_FF, D_MODEL)),
            pl.BlockSpec((1, D_MODEL), lambda i: (0, 0)),
        ],
        out_specs=out_specs,
        scratch_shapes=[pltpu.VMEM((tm, D_FF), BF16)],
        compiler_params=_cparams(("arbitrary",)),
        name="swiglu_halfstep",
    )(*xs, modt, g.reshape(1, D_MODEL), wg, wu, wd, final_norm.reshape(1, D_MODEL))


MLA_TM = 512


def _swap_halves(x):
    lane = lax.broadcasted_iota(jnp.int32, x.shape, 1)
    fwd = pltpu.roll(x, ROPE_PAD - AXIS_FREQS, axis=1)
    bwd = pltpu.roll(x, AXIS_FREQS, axis=1)
    return jnp.where(lane % (2 * AXIS_FREQS) < AXIS_FREQS, fwd, bwd)


def _rope(x, cos, sin):
    return x * cos + _swap_halves(x) * sin


def _kv_up(ckv, kpe_pad, wuk_ref, wuvt_ref, k_ref, vt_ref):
    c = ckv.astype(BF16)
    kn = jnp.dot(c, wuk_ref[...], preferred_element_type=F32).astype(BF16)
    vt = lax.dot_general(wuvt_ref[...], c, _NT, preferred_element_type=F32)
    vt_ref[...] = vt.astype(BF16)
    kp = kpe_pad.astype(BF16)
    for hd in range(N_HEADS):
        k_ref[:, hd * HEAD_PAD:hd * HEAD_PAD + QK_NOPE] = kn[:, hd * QK_NOPE:(hd + 1) * QK_NOPE]
        k_ref[:, hd * HEAD_PAD + QK_NOPE:(hd + 1) * HEAD_PAD] = kp


def _mla_proj_kernel(x_ref, mod_ref, g_ref, wdq_ref, qn_ref, wuq_ref, wdkv_ref, kvn_ref, wuk_ref,
                     wuvt_ref, cos_ref, sin_ref, q_ref, k_ref, vt_ref, ckv_ref, kpe_ref):
    h = _mod_rms(x_ref[...], g_ref[...], mod_ref, 3).astype(BF16)
    cq = _rms(jnp.dot(h, wdq_ref[...], preferred_element_type=F32), qn_ref[...]).astype(BF16)
    q = jnp.dot(cq, wuq_ref[...], preferred_element_type=F32)
    cos = cos_ref[...]
    sin = sin_ref[...]
    for hd in range(N_HEADS):
        lo = hd * HEAD_PAD
        q_ref[:, lo:lo + QK_NOPE] = (q[:, lo:lo + QK_NOPE] * Q_SCALE).astype(BF16)
        qr = _rope(q[:, lo + QK_NOPE:lo + HEAD_PAD], cos, sin)
        q_ref[:, lo + QK_NOPE:lo + HEAD_PAD] = (qr * Q_SCALE).astype(BF16)
    kv = jnp.dot(h, wdkv_ref[...], preferred_element_type=F32)
    ckv = _rms(kv[:, :KV_LORA], kvn_ref[...])
    kpe = _rope(kv[:, KV_LORA:], cos, sin)

    @pl.when(pl.program_id(0) < N_CTX // MLA_TM)
    def _():
        ckv_ref[...] = ckv.reshape(ckv_ref.shape)
        kpe_ref[...] = kpe[:, :QK_ROPE].reshape(kpe_ref.shape)

    _kv_up(ckv, kpe, wuk_ref, wuvt_ref, k_ref, vt_ref)


def _mla_proj(x, modt, l, g, wdq, qn, wuq, wdkv, kvn, wuk, wuvt, cos_t, sin_t):
    tm = MLA_TM
    n_ctx = N_CTX // tm
    per_batch = DEC_SEQ // tm

    def rope_idx(i):
        return jnp.where(i < n_ctx, per_batch, (i - n_ctx) % per_batch)

    full = lambda shape: pl.BlockSpec(shape, lambda i: (0, 0))
    rows = lambda w: pl.BlockSpec((tm, w), lambda i: (i, 0))
    state = lambda w: pl.BlockSpec((tm // SEQ, SEQ, w),
                                   lambda i: (jnp.minimum(i, n_ctx - 1), 0, 0))
    return pl.pallas_call(
        _mla_proj_kernel,
        out_shape=(
            jax.ShapeDtypeStruct((N_TOK, N_HEADS * HEAD_PAD), BF16),
            jax.ShapeDtypeStruct((N_TOK, N_HEADS * HEAD_PAD), BF16),
            jax.ShapeDtypeStruct((N_HEADS * V_DIM, N_TOK), BF16),
            jax.ShapeDtypeStruct((BATCH, SEQ, KV_LORA), F32),
            jax.ShapeDtypeStruct((BATCH, SEQ, QK_ROPE), F32),
        ),
        grid=(N_TOK // tm,),
        in_specs=[
            rows(D_MODEL),
            pl.BlockSpec((None, None, N_MOD, D_MODEL), lambda i: (l, _group(i, tm), 0, 0)),
            full((1, D_MODEL)),
            full((D_MODEL, Q_LORA)),
            full((1, Q_LORA)),
            full((Q_LORA, N_HEADS * HEAD_PAD)),
            full((D_MODEL, KV_LORA + ROPE_PAD)),
            full((1, KV_LORA)),
            full((KV_LORA, N_HEADS * QK_NOPE)),
            full((N_HEADS * V_DIM, KV_LORA)),
            pl.BlockSpec((tm, ROPE_PAD), lambda i: (rope_idx(i), 0)),
            pl.BlockSpec((tm, ROPE_PAD), lambda i: (rope_idx(i), 0)),
        ],
        out_specs=(rows(N_HEADS * HEAD_PAD), rows(N_HEADS * HEAD_PAD),
                   pl.BlockSpec((N_HEADS * V_DIM, tm), lambda i: (0, i)),
                   state(KV_LORA), state(QK_ROPE)),
        compiler_params=_cparams(("arbitrary",)),
        name="mla_projections",
    )(x, modt, g.reshape(1, D_MODEL), wdq, qn.reshape(1, Q_LORA), wuq, wdkv,
      kvn.reshape(1, KV_LORA), wuk, wuvt, cos_t, sin_t)


def _cache_kv_kernel(ckv_ref, kpe_ref, wuk_ref, wuvt_ref, k_ref, vt_ref):
    _kv_up(ckv_ref[...], kpe_ref[...], wuk_ref, wuvt_ref, k_ref, vt_ref)


def _cache_kv(ckv, kpe_pad, wuk, wuvt):
    n = DEC_BATCH * PAST_LEN
    tm = PAST_LEN
    full = lambda shape: pl.BlockSpec(shape, lambda i: (0, 0))
    rows = lambda w: pl.BlockSpec((tm, w), lambda i: (i, 0))
    return pl.pallas_call(
        _cache_kv_kernel,
        out_shape=(jax.ShapeDtypeStruct((n, N_HEADS * HEAD_PAD), BF16),
                   jax.ShapeDtypeStruct((N_HEADS * V_DIM, n), BF16)),
        grid=(n // tm,),
        in_specs=[rows(KV_LORA), rows(ROPE_PAD),
                  full((KV_LORA, N_HEADS * QK_NOPE)), full((N_HEADS * V_DIM, KV_LORA))],
        out_specs=(rows(N_HEADS * HEAD_PAD), pl.BlockSpec((N_HEADS * V_DIM, tm), lambda i: (0, i))),
        compiler_params=_cparams(("parallel",)),
        name="cache_kv_up",
    )(ckv, kpe_pad, wuk, wuvt)


_NT = (((1,), (1,)), ((), ()))


def _ctx_attn_kernel(q_ref, k_ref, vt_ref, o_ref):
    for hd in range(N_HEADS):
        q = q_ref[:, hd * HEAD_PAD:(hd + 1) * HEAD_PAD]
        k = k_ref[:, hd * HEAD_PAD:(hd + 1) * HEAD_PAD]
        st = lax.dot_general(k, q, _NT, preferred_element_type=F32)
        p = jnp.exp2(st - jnp.max(st, axis=0, keepdims=True))
        ot = jnp.dot(vt_ref[hd * V_DIM:(hd + 1) * V_DIM, :], p.astype(BF16),
                     preferred_element_type=F32)
        ot = ot / jnp.sum(p, axis=0, keepdims=True)
        o_ref[:, hd * V_DIM:(hd + 1) * V_DIM] = ot.T.astype(BF16)


def _ctx_attn(q, k, vt):
    rows = lambda w: pl.BlockSpec((SEQ, w), lambda b: (b, 0))
    return pl.pallas_call(
        _ctx_attn_kernel,
        out_shape=jax.ShapeDtypeStruct((N_CTX, N_HEADS * V_DIM), BF16),
        grid=(BATCH,),
        in_specs=[rows(N_HEADS * HEAD_PAD), rows(N_HEADS * HEAD_PAD),
                  pl.BlockSpec((N_HEADS * V_DIM, SEQ), lambda b: (0, b))],
        out_specs=rows(N_HEADS * V_DIM),
        compiler_params=_cparams(("parallel",)),
        name="context_attention",
    )(q, k, vt)


LAT_TQ = 512
LAT_KC = 512
LAT_KEYS = PAST_LEN + DEC_SEQ
SUBLANES = 8


def _lat_attn_kernel(q_ref, kc_ref, kl_ref, vct_ref, vlt_ref, o_ref, s_ref, m_ref):
    tq = q_ref.shape[0]
    kc = LAT_KC
    n_cache = PAST_LEN // kc
    n_chunks = LAT_KEYS // kc
    step = pl.program_id(2)

    @pl.when((pl.program_id(0) == 0) & (pl.program_id(1) == 0) & (step == 0))
    def _():
        s_ref[1] = jnp.zeros(s_ref.shape[1:], F32)
        m_ref[1] = jnp.zeros(m_ref.shape[1:], F32)

    def rows8(x, op):
        return op(x.reshape(kc // SUBLANES, SUBLANES, tq), axis=0)

    def body(cur):
        prev = 1 - cur
        q = q_ref[...]
        m = jnp.max(m_ref[prev], axis=0, keepdims=True)
        mpart = None
        lpart = jnp.zeros((SUBLANES, tq), F32)
        acc = jnp.zeros((V_DIM, tq), F32)
        for c in range(n_chunks):
            rows = slice(c * kc, (c + 1) * kc)
            if c < n_cache:
                k = kc_ref[rows, :]
                vt = vct_ref[:, rows]
            else:
                lat = slice((c - n_cache) * kc, (c - n_cache + 1) * kc)
                k = kl_ref[lat, :]
                vt = vlt_ref[:, lat]
            st = lax.dot_general(k, q, _NT, preferred_element_type=F32)
            s_ref[cur, rows, :] = st
            cm = rows8(st, jnp.max)
            mpart = cm if mpart is None else jnp.maximum(mpart, cm)
            p = jnp.exp2(s_ref[prev, rows, :] - m)
            lpart = lpart + rows8(p, jnp.sum)
            acc = acc + jnp.dot(vt, p.astype(BF16), preferred_element_type=F32)
        m_ref[cur] = mpart
        denom = jnp.sum(lpart, axis=0, keepdims=True)
        o_ref[...] = (acc / denom).T.astype(BF16)

    for parity in range(2):
        pl.when(step % 2 == parity)(functools.partial(body, parity))


def _lat_attn(q, k, vt, kc, vct):
    tq = LAT_TQ
    q_off = N_CTX // tq
    l_off = N_CTX // DEC_SEQ
    nq = DEC_SEQ // tq
    return pl.pallas_call(
        _lat_attn_kernel,
        out_shape=jax.ShapeDtypeStruct((N_LAT, N_HEADS * V_DIM), BF16),
        grid=(DEC_BATCH, N_HEADS, nq + 1),
        in_specs=[
            pl.BlockSpec((tq, HEAD_PAD),
                         lambda b, h, i: (q_off + b * nq + jnp.minimum(i, nq - 1), h)),
            pl.BlockSpec((PAST_LEN, HEAD_PAD), lambda b, h, i: (b, h)),
            pl.BlockSpec((DEC_SEQ, HEAD_PAD), lambda b, h, i: (l_off + b, h)),
            pl.BlockSpec((V_DIM, PAST_LEN), lambda b, h, i: (h, b)),
            pl.BlockSpec((V_DIM, DEC_SEQ), lambda b, h, i: (h, l_off + b)),
        ],
        out_specs=pl.BlockSpec((tq, V_DIM),
                               lambda b, h, i: (b * nq + jnp.maximum(i - 1, 0), h)),
        scratch_shapes=[pltpu.VMEM((2, LAT_KEYS, tq), F32), pltpu.VMEM((2, SUBLANES, tq), F32)],
        compiler_params=_cparams(("arbitrary", "arbitrary", "arbitrary")),
        name="latent_attention",
    )(q, kc, k, vct, vt)


OPROJ_TM = 1024


def _oproj_kernel(x_ref, oc_ref, ol_ref, w_ref, mod_ref, y_ref):
    o = jnp.where(pl.program_id(0) < N_CTX // OPROJ_TM, oc_ref[...], ol_ref[...])
    y_ref[...] = x_ref[...] + mod_ref[5:6, :] * jnp.dot(o, w_ref[...], preferred_element_type=F32)


def _oproj(x, o_ctx, o_lat, w, modt, l):
    tm = OPROJ_TM
    n_ctx = N_CTX // tm
    return pl.pallas_call(
        _oproj_kernel,
        out_shape=jax.ShapeDtypeStruct((N_TOK, D_MODEL), F32),
        grid=(N_TOK // tm,),
        in_specs=[
            pl.BlockSpec((tm, D_MODEL), lambda i: (i, 0)),
            pl.BlockSpec((tm, N_HEADS * V_DIM), lambda i: (jnp.minimum(i, n_ctx - 1), 0)),
            pl.BlockSpec((tm, N_HEADS * V_DIM), lambda i: (jnp.maximum(i - n_ctx, 0), 0)),
            pl.BlockSpec((N_HEADS * V_DIM, D_MODEL), lambda i: (0, 0)),
            pl.BlockSpec((None, None, N_MOD, D_MODEL), lambda i: (l, _group(i, tm), 0, 0)),
        ],
        out_specs=pl.BlockSpec((tm, D_MODEL), lambda i: (i, 0)),
        compiler_params=_cparams(("parallel",)),
        name="attn_out_proj",
    )(x, o_ctx, o_lat, w, modt)


FCH_TM = 1024


def _fourier_ch_kernel(x_ref, mod_ref, g_ref, cs_ref, a_ref, b_ref):
    h = _mod_rms(x_ref[...], g_ref[...], mod_ref, 3).astype(BF16)
    for gi in range(N_FGROUPS):
        lo = gi * FGROUP_DIM
        r = jnp.dot(h[:, lo:lo + FGROUP_DIM], cs_ref[...], preferred_element_type=F32)
        a_ref[:, lo:lo + FGROUP_DIM] = r[:, :FGROUP_DIM].astype(BF16)
        b_ref[:, lo:lo + FGROUP_DIM] = r[:, FGROUP_DIM:].astype(BF16)


def _fourier_ch(x, modt, l, g, ch_tab):
    tm = FCH_TM
    rows = pl.BlockSpec((tm, D_MODEL), lambda i: (i, 0))
    return pl.pallas_call(
        _fourier_ch_kernel,
        out_shape=(jax.ShapeDtypeStruct((N_TOK, D_MODEL), BF16),
                   jax.ShapeDtypeStruct((N_TOK, D_MODEL), BF16)),
        grid=(N_TOK // tm,),
        in_specs=[
            rows,
            pl.BlockSpec((None, None, N_MOD, D_MODEL), lambda i: (l, _group(i, tm), 0, 0)),
            pl.BlockSpec((1, D_MODEL), lambda i: (0, 0)),
            pl.BlockSpec((FGROUP_DIM, 2 * FGROUP_DIM), lambda i: (0, 0)),
        ],
        out_specs=(rows, rows),
        compiler_params=_cparams(("parallel",)),
        name="fourier_channel_dft",
    )(x, modt, g.reshape(1, D_MODEL), ch_tab)


def _fourier_epilogue(f, x_ref, w_ref, bias_ref, mod_ref):
    mixed = jnp.dot(f.astype(BF16), w_ref[...], preferred_element_type=F32) + bias_ref[...]
    return x_ref[...] + mod_ref[5:6, :] * mixed


def _fourier_ctx_kernel(a_ref, b_ref, c_ref, s_ref, x_ref, w_ref, bias_ref, mod_ref, y_ref):
    f = jnp.dot(c_ref[...], a_ref[...], preferred_element_type=F32)
    f = f + jnp.dot(s_ref[...], b_ref[...], preferred_element_type=F32)
    y_ref[...] = _fourier_epilogue(f, x_ref, w_ref, bias_ref, mod_ref)


def _fourier_ctx(a, b, cos_t, nsin_t, x, w, bias, modt, l):
    rows = lambda dt: pl.BlockSpec((SEQ, D_MODEL), lambda i: (i, 0))
    full = lambda shape: pl.BlockSpec(shape, lambda i: (0, 0))
    return pl.pallas_call(
        _fourier_ctx_kernel,
        out_shape=jax.ShapeDtypeStruct((N_CTX, D_MODEL), F32),
        grid=(BATCH,),
        in_specs=[rows(BF16), rows(BF16), full((SEQ, SEQ)), full((SEQ, SEQ)), rows(F32),
                  full((D_MODEL, D_MODEL)), full((1, D_MODEL)),
                  pl.BlockSpec((None, None, N_MOD, D_MODEL), lambda i: (l, 0, 0, 0))],
        out_specs=rows(F32),
        compiler_params=_cparams(("parallel",)),
        name="fourier_context_positions",
    )(a, b, cos_t, nsin_t, x, w, bias.reshape(1, D_MODEL), modt)


FLAT_TM = 1024
FLAT_TK = 1024


def _fourier_lat_kernel(a_ref, b_ref, c_ref, s_ref, x_ref, w_ref, bias_ref, mod_ref, y_ref, acc_ref):
    k = pl.program_id(2)

    @pl.when(k == 0)
    def _():
        acc_ref[...] = jnp.zeros_like(acc_ref)

    part = jnp.dot(c_ref[...], a_ref[...], preferred_element_type=F32)
    part = part + jnp.dot(s_ref[...], b_ref[...], preferred_element_type=F32)
    acc_ref[...] += part

    @pl.when(k == pl.num_programs(2) - 1)
    def _():
        y_ref[...] = _fourier_epilogue(acc_ref[...], x_ref, w_ref, bias_ref, mod_ref)


def _fourier_lat(a, b, cos_t, nsin_t, x, w, bias, modt, l):
    tm, tk = FLAT_TM, FLAT_TK
    ni = DEC_SEQ // tm
    nk = DEC_SEQ // tk
    r_off = N_CTX // tm
    k_off = N_CTX // tk
    src = pl.BlockSpec((tk, D_MODEL), lambda bb, i, k: (k_off + bb * nk + k, 0))
    tab = pl.BlockSpec((tm, tk), lambda bb, i, k: (i, k))
    xrow = pl.BlockSpec((tm, D_MODEL), lambda bb, i, k: (r_off + bb * ni + i, 0))
    return pl.pallas_call(
        _fourier_lat_kernel,
        out_shape=jax.ShapeDtypeStruct((N_LAT, D_MODEL), F32),
        grid=(DEC_BATCH, ni, nk),
        in_specs=[src, src, tab, tab, xrow,
                  pl.BlockSpec((D_MODEL, D_MODEL), lambda bb, i, k: (0, 0)),
                  pl.BlockSpec((1, D_MODEL), lambda bb, i, k: (0, 0)),
                  pl.BlockSpec((None, None, N_MOD, D_MODEL), lambda bb, i, k: (l, 1 + bb, 0, 0))],
        out_specs=pl.BlockSpec((tm, D_MODEL), lambda bb, i, k: (bb * ni + i, 0)),
        scratch_shapes=[pltpu.VMEM((tm, D_MODEL), F32)],
        compiler_params=_cparams(("parallel", "parallel", "arbitrary")),
        name="fourier_latent_positions",
    )(a, b, cos_t, nsin_t, x, w, bias.reshape(1, D_MODEL), modt)


TAB_R = 64
TAB_TM = 256


def _dft_table_kernel(ar_ref, ai_ref, br_ref, bi_ref, cos_ref, nsin_ref):
    br = br_ref[...]
    bi = bi_ref[...]
    for r in range(TAB_TM // TAB_R):
        ar = ar_ref[r]
        ai = ai_ref[r]
        rows = slice(r * TAB_R, (r + 1) * TAB_R)
        cos_ref[rows, :] = (ar * br - ai * bi).astype(BF16)
        nsin_ref[rows, :] = (ai * br + ar * bi).astype(BF16)


def _latent_dft_tables():
    n = DEC_SEQ
    assert n == TAB_R * TAB_R
    idx = jnp.arange(TAB_R, dtype=jnp.int32)
    prod = idx[:, None] * idx[None, :]
    ang_c = (prod % TAB_R).astype(F32) * (2.0 * math.pi / TAB_R)
    ang_f = prod.astype(F32) * (2.0 * math.pi / n)
    cr, ci = jnp.cos(ang_c), -jnp.sin(ang_c)
    scale = 1.0 / math.sqrt(n)
    fr, fi = jnp.cos(ang_f) * scale, jnp.sin(ang_f) * (-scale)
    ar = jnp.tile(cr, (1, TAB_R)).reshape(TAB_R, 1, n)
    ai = jnp.tile(ci, (1, TAB_R)).reshape(TAB_R, 1, n)
    br = (cr[:, :, None] * fr[:, None, :] - ci[:, :, None] * fi[:, None, :]).reshape(TAB_R, n)
    bi = (cr[:, :, None] * fi[:, None, :] + ci[:, :, None] * fr[:, None, :]).reshape(TAB_R, n)
    per = TAB_TM // TAB_R
    coarse = pl.BlockSpec((per, 1, n), lambda i: (i, 0, 0))
    fine = pl.BlockSpec((TAB_R, n), lambda i: (0, 0))
    out = pl.BlockSpec((TAB_TM, n), lambda i: (i, 0))
    return pl.pallas_call(
        _dft_table_kernel,
        out_shape=(jax.ShapeDtypeStruct((n, n), BF16), jax.ShapeDtypeStruct((n, n), BF16)),
        grid=(n // TAB_TM,),
        in_specs=[coarse, coarse, fine, fine],
        out_specs=(out, out),
        compiler_params=_cparams(("parallel",)),
        name="latent_dft_tables",
    )(ar, ai, br, bi)


def _dft_tables(n, scale):
    idx = jnp.arange(n, dtype=jnp.int32)
    ang = ((idx[:, None] * idx[None, :]) % n).astype(F32) * (2.0 * math.pi / n)
    return (jnp.cos(ang) * scale).astype(BF16), (jnp.sin(ang) * (-scale)).astype(BF16)


def _rope_tables(tm):
    rows = DEC_SEQ // GRID_W
    row = jnp.repeat(jnp.arange(rows, dtype=F32), GRID_W)
    col = jnp.tile(jnp.arange(GRID_W, dtype=F32), rows)
    inv = 1.0 / (ROPE_THETA ** (jnp.arange(AXIS_FREQS, dtype=F32) / AXIS_FREQS))
    ar = row[:, None] * inv
    ac = col[:, None] * inv
    pad = jnp.zeros((DEC_SEQ, ROPE_PAD - QK_ROPE), F32)
    cos = jnp.concatenate([jnp.cos(ar), jnp.cos(ar), jnp.cos(ac), jnp.cos(ac), pad + 1.0], axis=1)
    sin = jnp.concatenate([-jnp.sin(ar), jnp.sin(ar), -jnp.sin(ac), jnp.sin(ac), pad], axis=1)
    cos = jnp.concatenate([cos, jnp.ones((tm, ROPE_PAD), F32)], axis=0)
    sin = jnp.concatenate([sin, jnp.zeros((tm, ROPE_PAD), F32)], axis=0)
    return cos, sin


def _pad_heads(w, part, width):
    k = w.shape[0]
    w = w.reshape(k, N_HEADS, -1)[:, :, part]
    return jnp.pad(w, ((0, 0), (0, 0), (0, width - w.shape[-1])))


def kernel(x_prompt, x_sample, cache_ckv, cache_kpe, c, c_ctx, w_mod, b_mod, norm_g, ffn_wg, ffn_wu,
           ffn_wd, mla_w_dq, mla_q_norm, mla_w_uq, mla_w_dkv, mla_kv_norm, mla_w_ukv, mla_w_o,
           fourier_w, fourier_b, final_norm):
    xs = (x_prompt.reshape(N_CTX, D_MODEL), x_sample.reshape(N_LAT, D_MODEL))
    cond = jnp.concatenate([c_ctx[None, :], c, jnp.zeros((N_GROUPS - 1 - DEC_BATCH, D_MODEL), F32)], axis=0)
    modt = _modulation(cond, w_mod, b_mod)

    wg = ffn_wg.astype(BF16)
    wu = ffn_wu.astype(BF16)
    wd = ffn_wd.astype(BF16)

    rope_cos, rope_sin = _rope_tables(MLA_TM)
    ch_cos, ch_nsin = _dft_tables(FGROUP_DIM, 1.0 / math.sqrt(FGROUP_DIM))
    ch_tab = jnp.concatenate([ch_cos, -ch_nsin], axis=1)
    ctx_cos, ctx_nsin = _dft_tables(SEQ, 1.0 / math.sqrt(SEQ))
    lat_cos, lat_nsin = _latent_dft_tables()

    ckv_states = []
    kpe_states = []
    for l in range(DEPTH):
        j = l // 2
        x = _ffn(xs if l == 0 else (x,), modt, l, 0, norm_g[l, 0], wg, wu, wd, final_norm, False)
        if l % 2 == 0:
            wuq = mla_w_uq[j]
            wuq_p = jnp.concatenate(
                [_pad_heads(wuq, slice(0, QK_NOPE), QK_NOPE),
                 _pad_heads(wuq, slice(QK_NOPE, QK_NOPE + QK_ROPE), ROPE_PAD)], axis=-1)
            wuq_p = wuq_p.reshape(Q_LORA, N_HEADS * HEAD_PAD).astype(BF16)
            wdkv_p = jnp.pad(mla_w_dkv[j], ((0, 0), (0, ROPE_PAD - QK_ROPE))).astype(BF16)
            wukv = mla_w_ukv[j]
            wuk = _pad_heads(wukv, slice(0, QK_NOPE), QK_NOPE).reshape(KV_LORA, -1).astype(BF16)
            wuvt = _pad_heads(wukv, slice(QK_NOPE, QK_NOPE + V_DIM), V_DIM).reshape(KV_LORA, -1).T.astype(BF16)
            q, k, vt, new_ckv, new_kpe = _mla_proj(
                x, modt, l, norm_g[l, 1], mla_w_dq[j].astype(BF16), mla_q_norm[j], wuq_p, wdkv_p,
                mla_kv_norm[j], wuk, wuvt, rope_cos, rope_sin)
            ckv_states.append(new_ckv)
            kpe_states.append(new_kpe)
            kc, vct = _cache_kv(
                cache_ckv[:, j].reshape(DEC_BATCH * PAST_LEN, KV_LORA),
                jnp.pad(cache_kpe[:, j].reshape(DEC_BATCH * PAST_LEN, QK_ROPE),
                        ((0, 0), (0, ROPE_PAD - QK_ROPE))),
                wuk, wuvt)
            o_ctx = _ctx_attn(q, k, vt)
            o_lat = _lat_attn(q, k, vt, kc, vct)
            mixed = (_oproj(x, o_ctx, o_lat, mla_w_o[j].astype(BF16), modt, l),)
        else:
            a, b = _fourier_ch(x, modt, l, norm_g[l, 1], ch_tab)
            fw = fourier_w[j].astype(BF16)
            mixed = (_fourier_ctx(a, b, ctx_cos, ctx_nsin, x, fw, fourier_b[j], modt, l),
                     _fourier_lat(a, b, lat_cos, lat_nsin, x, fw, fourier_b[j], modt, l))
        x = _ffn(mixed, modt, l, 1, norm_g[l, 2], wg, wu, wd, final_norm, l == DEPTH - 1)

    y_ctx, y_lat = x
    return (y_ctx.reshape(BATCH, SEQ, D_MODEL), y_lat.reshape(DEC_BATCH, DEC_SEQ, D_MODEL),
            jnp.stack(ckv_states, axis=1), jnp.stack(kpe_states, axis=1))
```

```python
import functools
import math

import jax
import jax.numpy as jnp
from jax import lax
from jax.experimental import pallas as pl
from jax.experimental.pallas import tpu as pltpu

F32 = jnp.float32
BF16 = jnp.bfloat16

D_MODEL = 1024
BATCH = 32
SEQ = 256
DEPTH = 4
DEC_BATCH = 2
DEC_SEQ = 4096
PAST_LEN = 512
GRID_W = 64
N_HEADS = 8
QK_NOPE = 128
QK_ROPE = 64
V_DIM = 128
Q_LORA = 512
KV_LORA = 256
AXIS_FREQS = QK_ROPE // 4
ROPE_THETA = 10000.0
N_FGROUPS = 4
FGROUP_DIM = D_MODEL // N_FGROUPS
D_FF = 2816
N_MOD = 9
EPS = 1e-6
ATTN_SCALE = 1.0 / math.sqrt(QK_NOPE + QK_ROPE)
Q_SCALE = ATTN_SCALE * math.log2(math.e)

N_CTX = BATCH * SEQ
N_LAT = DEC_BATCH * DEC_SEQ
N_TOK = N_CTX + N_LAT
N_GROUPS = 8
HEAD_PAD = 256
ROPE_PAD = 128
VMEM_LIMIT = 56 * 1024 * 1024


def _cparams(sem):
    return pltpu.CompilerParams(dimension_semantics=sem, vmem_limit_bytes=VMEM_LIMIT)


def _group(i, tm):
    n_ctx = N_CTX // tm
    per_batch = DEC_SEQ // tm
    return jnp.where(i < n_ctx, 0, 1 + (i - n_ctx) // per_batch)


def _rms(x, g):
    return x * lax.rsqrt(jnp.mean(x * x, axis=-1, keepdims=True) + EPS) * g


def _mod_rms(x, g, mod, k):
    gain = g * (1.0 + mod[k + 1:k + 2, :])
    return x * lax.rsqrt(jnp.mean(x * x, axis=-1, keepdims=True) + EPS) * gain + mod[k:k + 1, :]


def _mod_kernel(c_ref, w_ref, b_ref, o_ref):
    c = c_ref[...]
    s = (c * jax.nn.sigmoid(c)).astype(BF16)
    o_ref[...] = jnp.dot(s, w_ref[...].astype(BF16), preferred_element_type=F32) + b_ref[...]


def _modulation(cond, w_mod, b_mod):
    out = pl.pallas_call(
        _mod_kernel,
        out_shape=jax.ShapeDtypeStruct((DEPTH, N_MOD, N_GROUPS, D_MODEL), F32),
        grid=(DEPTH, N_MOD),
        in_specs=[
            pl.BlockSpec((N_GROUPS, D_MODEL), lambda l, k: (0, 0)),
            pl.BlockSpec((None, D_MODEL, D_MODEL), lambda l, k: (l, 0, k)),
            pl.BlockSpec((None, None, 1, D_MODEL), lambda l, k: (l, k, 0, 0)),
        ],
        out_specs=pl.BlockSpec((None, None, N_GROUPS, D_MODEL), lambda l, k: (l, k, 0, 0)),
        compiler_params=_cparams(("arbitrary", "arbitrary")),
        name="adaln_modulation",
    )(cond, w_mod, b_mod.reshape(DEPTH, N_MOD, 1, D_MODEL))
    return out.transpose(0, 2, 1, 3)


FFN_TM = 512
FFN_TF = 256
FFN_CTX_TILES = N_CTX // FFN_TM


def _ffn_kernel(*refs, k0, split_in, final):
    n_x = 2 if split_in else 1
    x_refs, (mod_ref, g_ref, wg_ref, wu_ref, wd_ref, fn_ref) = refs[:n_x], refs[n_x:n_x + 6]
    n_out = 2 if final else 1
    out_refs, (a_ref,) = refs[n_x + 6:n_x + 6 + n_out], refs[n_x + 6 + n_out:]
    is_ctx = pl.program_id(0) < FFN_CTX_TILES

    if split_in:
        x = jnp.where(is_ctx, x_refs[0][...], x_refs[1][...])
    else:
        x = x_refs[0][...]
    h = _mod_rms(x, g_ref[...], mod_ref, k0).astype(BF16)
    for c in range(D_FF // FFN_TF):
        cols = slice(c * FFN_TF, (c + 1) * FFN_TF)
        g = jnp.dot(h, wg_ref[:, cols], preferred_element_type=F32)
        u = jnp.dot(h, wu_ref[:, cols], preferred_element_type=F32)
        a_ref[:, cols] = (g * jax.nn.sigmoid(g) * u).astype(BF16)
    y = jnp.dot(a_ref[...], wd_ref[...], preferred_element_type=F32)
    y = x + (0.5 * mod_ref[k0 + 2:k0 + 3, :]) * y
    if final:
        y = _rms(y, fn_ref[...])

        @pl.when(is_ctx)
        def _():
            out_refs[0][...] = y

        @pl.when(jnp.logical_not(is_ctx))
        def _():
            out_refs[1][...] = y
    else:
        out_refs[0][...] = y


def _ffn(xs, modt, l, half, g, wg, wu, wd, final_norm, final):
    tm = FFN_TM
    k0 = 6 * half
    split_in = len(xs) == 2
    rows = pl.BlockSpec((tm, D_MODEL), lambda i: (i, 0))
    ctx_rows = pl.BlockSpec((tm, D_MODEL), lambda i: (jnp.minimum(i, FFN_CTX_TILES - 1), 0))
    lat_rows = pl.BlockSpec((tm, D_MODEL), lambda i: (jnp.maximum(i - FFN_CTX_TILES, 0), 0))
    resident = lambda shape: pl.BlockSpec((None, None) + shape, lambda i: (l, half, 0, 0),
                                          pipeline_mode=pl.Buffered(1))
    if final:
        out_shape = (jax.ShapeDtypeStruct((N_CTX, D_MODEL), F32),
                     jax.ShapeDtypeStruct((N_LAT, D_MODEL), F32))
        out_specs = (ctx_rows, lat_rows)
    else:
        out_shape = jax.ShapeDtypeStruct((N_TOK, D_MODEL), F32)
        out_specs = rows
    return pl.pallas_call(
        functools.partial(_ffn_kernel, k0=k0, split_in=split_in, final=final),
        out_shape=out_shape,
        grid=(N_TOK // tm,),
        in_specs=([ctx_rows, lat_rows] if split_in else [rows]) + [
            pl.BlockSpec((None, None, N_MOD, D_MODEL), lambda i: (l, _group(i, tm), 0, 0)),
            pl.BlockSpec((1, D_MODEL), lambda i: (0, 0)),
            resident((D_MODEL, D_FF)),
            resident((D_MODEL, D_FF)),
            resident((D_FF, D_MODEL)),
            pl.BlockSpec((1, D_MODEL), lambda i: (0, 0)),
        ],
        out_specs=out_specs,
        scratch_shapes=[pltpu.VMEM((tm, D_FF), BF16)],
        compiler_params=_cparams(("arbitrary",)),
        name="swiglu_halfstep",
    )(*xs, modt, g.reshape(1, D_MODEL), wg, wu, wd, final_norm.reshape(1, D_MODEL))


MLA_TM = 512


def _swap_halves(x):
    lane = lax.broadcasted_iota(jnp.int32, x.shape, 1)
    fwd = pltpu.roll(x, ROPE_PAD - AXIS_FREQS, axis=1)
    bwd = pltpu.roll(x, AXIS_FREQS, axis=1)
    return jnp.where(lane % (2 * AXIS_FREQS) < AXIS_FREQS, fwd, bwd)


def _rope(x, cos, sin):
    return x * cos + _swap_halves(x) * sin


def _kv_up(ckv, kpe_pad, wuk_ref, wuvt_ref, k_ref, vt_ref):
    c = ckv.astype(BF16)
    kn = jnp.dot(c, wuk_ref[...], preferred_element_type=F32).astype(BF16)
    vt = lax.dot_general(wuvt_ref[...], c, _NT, preferred_element_type=F32)
    vt_ref[...] = vt.astype(BF16)
    kp = kpe_pad.astype(BF16)
    for hd in range(N_HEADS):
        k_ref[:, hd * HEAD_PAD:hd * HEAD_PAD + QK_NOPE] = kn[:, hd * QK_NOPE:(hd + 1) * QK_NOPE]
        k_ref[:, hd * HEAD_PAD + QK_NOPE:(hd + 1) * HEAD_PAD] = kp


def _mla_proj_kernel(x_ref, mod_ref, g_ref, wdq_ref, qn_ref, wuq_ref, wdkv_ref, kvn_ref, wuk_ref,
                     wuvt_ref, cos_ref, sin_ref, q_ref, k_ref, vt_ref, ckv_ref, kpe_ref):
    h = _mod_rms(x_ref[...], g_ref[...], mod_ref, 3).astype(BF16)
    cq = _rms(jnp.dot(h, wdq_ref[...], preferred_element_type=F32), qn_ref[...]).astype(BF16)
    q = jnp.dot(cq, wuq_ref[...], preferred_element_type=F32)
    cos = cos_ref[...]
    sin = sin_ref[...]
    for hd in range(N_HEADS):
        lo = hd * HEAD_PAD
        q_ref[:, lo:lo + QK_NOPE] = (q[:, lo:lo + QK_NOPE] * Q_SCALE).astype(BF16)
        qr = _rope(q[:, lo + QK_NOPE:lo + HEAD_PAD], cos, sin)
        q_ref[:, lo + QK_NOPE:lo + HEAD_PAD] = (qr * Q_SCALE).astype(BF16)
    kv = jnp.dot(h, wdkv_ref[...], preferred_element_type=F32)
    ckv = _rms(kv[:, :KV_LORA], kvn_ref[...])
    kpe = _rope(kv[:, KV_LORA:], cos, sin)

    @pl.when(pl.program_id(0) < N_CTX // MLA_TM)
    def _():
        ckv_ref[...] = ckv.reshape(ckv_ref.shape)
        kpe_ref[...] = kpe[:, :QK_ROPE].reshape(kpe_ref.shape)

    _kv_up(ckv, kpe, wuk_ref, wuvt_ref, k_ref, vt_ref)


def _mla_proj(x, modt, l, g, wdq, qn, wuq, wdkv, kvn, wuk, wuvt, cos_t, sin_t):
    tm = MLA_TM
    n_ctx = N_CTX // tm
    per_batch = DEC_SEQ // tm

    def rope_idx(i):
        return jnp.where(i < n_ctx, per_batch, (i - n_ctx) % per_batch)

    full = lambda shape: pl.BlockSpec(shape, lambda i: (0, 0))
    rows = lambda w: pl.BlockSpec((tm, w), lambda i: (i, 0))
    state = lambda w: pl.BlockSpec((tm // SEQ, SEQ, w),
                                   lambda i: (jnp.minimum(i, n_ctx - 1), 0, 0))
    return pl.pallas_call(
        _mla_proj_kernel,
        out_shape=(
            jax.ShapeDtypeStruct((N_TOK, N_HEADS * HEAD_PAD), BF16),
            jax.ShapeDtypeStruct((N_TOK, N_HEADS * HEAD_PAD), BF16),
            jax.ShapeDtypeStruct((N_HEADS * V_DIM, N_TOK), BF16),
            jax.ShapeDtypeStruct((BATCH, SEQ, KV_LORA), F32),
            jax.ShapeDtypeStruct((BATCH, SEQ, QK_ROPE), F32),
        ),
        grid=(N_TOK // tm,),
        in_specs=[
            rows(D_MODEL),
            pl.BlockSpec((None, None, N_MOD, D_MODEL), lambda i: (l, _group(i, tm), 0, 0)),
            full((1, D_MODEL)),
            full((D_MODEL, Q_LORA)),
            full((1, Q_LORA)),
            full((Q_LORA, N_HEADS * HEAD_PAD)),
            full((D_MODEL, KV_LORA + ROPE_PAD)),
            full((1, KV_LORA)),
            full((KV_LORA, N_HEADS * QK_NOPE)),
            full((N_HEADS * V_DIM, KV_LORA)),
            pl.BlockSpec((tm, ROPE_PAD), lambda i: (rope_idx(i), 0)),
            pl.BlockSpec((tm, ROPE_PAD), lambda i: (rope_idx(i), 0)),
        ],
        out_specs=(rows(N_HEADS * HEAD_PAD), rows(N_HEADS * HEAD_PAD),
                   pl.BlockSpec((N_HEADS * V_DIM, tm), lambda i: (0, i)),
                   state(KV_LORA), state(QK_ROPE)),
        compiler_params=_cparams(("arbitrary",)),
        name="mla_projections",
    )(x, modt, g.reshape(1, D_MODEL), wdq, qn.reshape(1, Q_LORA), wuq, wdkv,
      kvn.reshape(1, KV_LORA), wuk, wuvt, cos_t, sin_t)


def _cache_kv_kernel(ckv_ref, kpe_ref, wuk_ref, wuvt_ref, k_ref, vt_ref):
    _kv_up(ckv_ref[...], kpe_ref[...], wuk_ref, wuvt_ref, k_ref, vt_ref)


def _cache_kv(ckv, kpe_pad, wuk, wuvt):
    n = DEC_BATCH * PAST_LEN
    tm = PAST_LEN
    full = lambda shape: pl.BlockSpec(shape, lambda i: (0, 0))
    rows = lambda w: pl.BlockSpec((tm, w), lambda i: (i, 0))
    return pl.pallas_call(
        _cache_kv_kernel,
        out_shape=(jax.ShapeDtypeStruct((n, N_HEADS * HEAD_PAD), BF16),
                   jax.ShapeDtypeStruct((N_HEADS * V_DIM, n), BF16)),
        grid=(n // tm,),
        in_specs=[rows(KV_LORA), rows(ROPE_PAD),
                  full((KV_LORA, N_HEADS * QK_NOPE)), full((N_HEADS * V_DIM, KV_LORA))],
        out_specs=(rows(N_HEADS * HEAD_PAD), pl.BlockSpec((N_HEADS * V_DIM, tm), lambda i: (0, i))),
        compiler_params=_cparams(("parallel",)),
        name="cache_kv_up",
    )(ckv, kpe_pad, wuk, wuvt)


_NT = (((1,), (1,)), ((), ()))


def _ctx_attn_kernel(q_ref, k_ref, vt_ref, o_ref):
    for hd in range(N_HEADS):
        q = q_ref[:, hd * HEAD_PAD:(hd + 1) * HEAD_PAD]
        k = k_ref[:, hd * HEAD_PAD:(hd + 1) * HEAD_PAD]
        st = lax.dot_general(k, q, _NT, preferred_element_type=F32)
        p = jnp.exp2(st - jnp.max(st, axis=0, keepdims=True))
        ot = jnp.dot(vt_ref[hd * V_DIM:(hd + 1) * V_DIM, :], p.astype(BF16),
                     preferred_element_type=F32)
        ot = ot / jnp.sum(p, axis=0, keepdims=True)
        o_ref[:, hd * V_DIM:(hd + 1) * V_DIM] = ot.T.astype(BF16)


def _ctx_attn(q, k, vt):
    rows = lambda w: pl.BlockSpec((SEQ, w), lambda b: (b, 0))
    return pl.pallas_call(
        _ctx_attn_kernel,
        out_shape=jax.ShapeDtypeStruct((N_CTX, N_HEADS * V_DIM), BF16),
        grid=(BATCH,),
        in_specs=[rows(N_HEADS * HEAD_PAD), rows(N_HEADS * HEAD_PAD),
                  pl.BlockSpec((N_HEADS * V_DIM, SEQ), lambda b: (0, b))],
        out_specs=rows(N_HEADS * V_DIM),
        compiler_params=_cparams(("parallel",)),
        name="context_attention",
    )(q, k, vt)


LAT_TQ = 256
LAT_KC = 512
LAT_KEYS = PAST_LEN + DEC_SEQ
SUBLANES = 8


def _lat_attn_kernel(q_ref, kc_ref, kl_ref, vct_ref, vlt_ref, o_ref, s_ref, m_ref):
    tq = q_ref.shape[0]
    kc = LAT_KC
    n_cache = PAST_LEN // kc
    n_chunks = LAT_KEYS // kc
    step = pl.program_id(2)

    @pl.when((pl.program_id(0) == 0) & (pl.program_id(1) == 0) & (step == 0))
    def _():
        s_ref[1] = jnp.zeros(s_ref.shape[1:], F32)
        m_ref[1] = jnp.zeros(m_ref.shape[1:], F32)

    def rows8(x, op):
        return op(x.reshape(kc // SUBLANES, SUBLANES, tq), axis=0)

    def body(cur):
        prev = 1 - cur
        q = q_ref[...]
        m = jnp.max(m_ref[prev], axis=0, keepdims=True)
        mpart = None
        lpart = jnp.zeros((SUBLANES, tq), F32)
        acc = jnp.zeros((V_DIM, tq), F32)
        for c in range(n_chunks):
            rows = slice(c * kc, (c + 1) * kc)
            if c < n_cache:
                k = kc_ref[rows, :]
                vt = vct_ref[:, rows]
            else:
                lat = slice((c - n_cache) * kc, (c - n_cache + 1) * kc)
                k = kl_ref[lat, :]
                vt = vlt_ref[:, lat]
            st = lax.dot_general(k, q, _NT, preferred_element_type=F32)
            s_ref[cur, rows, :] = st
            cm = rows8(st, jnp.max)
            mpart = cm if mpart is None else jnp.maximum(mpart, cm)
            p = jnp.exp2(s_ref[prev, rows, :] - m)
            lpart = lpart + rows8(p, jnp.sum)
            acc = acc + jnp.dot(vt, p.astype(BF16), preferred_element_type=F32)
        m_ref[cur] = mpart
        denom = jnp.sum(lpart, axis=0, keepdims=True)
        o_ref[...] = (acc / denom).T.astype(BF16)

    for parity in range(2):
        pl.when(step % 2 == parity)(functools.partial(body, parity))


def _lat_attn(q, k, vt, kc, vct):
    tq = LAT_TQ
    q_off = N_CTX // tq
    l_off = N_CTX // DEC_SEQ
    nq = DEC_SEQ // tq
    return pl.pallas_call(
        _lat_attn_kernel,
        out_shape=jax.ShapeDtypeStruct((N_LAT, N_HEADS * V_DIM), BF16),
        grid=(DEC_BATCH, N_HEADS, nq + 1),
        in_specs=[
            pl.BlockSpec((tq, HEAD_PAD),
                         lambda b, h, i: (q_off + b * nq + jnp.minimum(i, nq - 1), h)),
            pl.BlockSpec((PAST_LEN, HEAD_PAD), lambda b, h, i: (b, h)),
            pl.BlockSpec((DEC_SEQ, HEAD_PAD), lambda b, h, i: (l_off + b, h)),
            pl.BlockSpec((V_DIM, PAST_LEN), lambda b, h, i: (h, b)),
            pl.BlockSpec((V_DIM, DEC_SEQ), lambda b, h, i: (h, l_off + b)),
        ],
        out_specs=pl.BlockSpec((tq, V_DIM),
                               lambda b, h, i: (b * nq + jnp.maximum(i - 1, 0), h)),
        scratch_shapes=[pltpu.VMEM((2, LAT_KEYS, tq), F32), pltpu.VMEM((2, SUBLANES, tq), F32)],
        compiler_params=_cparams(("arbitrary", "arbitrary", "arbitrary")),
        name="latent_attention",
    )(q, kc, k, vct, vt)


OPROJ_TM = 1024


def _oproj_kernel(x_ref, oc_ref, ol_ref, w_ref, mod_ref, y_ref):
    o = jnp.where(pl.program_id(0) < N_CTX // OPROJ_TM, oc_ref[...], ol_ref[...])
    y_ref[...] = x_ref[...] + mod_ref[5:6, :] * jnp.dot(o, w_ref[...], preferred_element_type=F32)


def _oproj(x, o_ctx, o_lat, w, modt, l):
    tm = OPROJ_TM
    n_ctx = N_CTX // tm
    return pl.pallas_call(
        _oproj_kernel,
        out_shape=jax.ShapeDtypeStruct((N_TOK, D_MODEL), F32),
        grid=(N_TOK // tm,),
        in_specs=[
            pl.BlockSpec((tm, D_MODEL), lambda i: (i, 0)),
            pl.BlockSpec((tm, N_HEADS * V_DIM), lambda i: (jnp.minimum(i, n_ctx - 1), 0)),
            pl.BlockSpec((tm, N_HEADS * V_DIM), lambda i: (jnp.maximum(i - n_ctx, 0), 0)),
            pl.BlockSpec((N_HEADS * V_DIM, D_MODEL), lambda i: (0, 0)),
            pl.BlockSpec((None, None, N_MOD, D_MODEL), lambda i: (l, _group(i, tm), 0, 0)),
        ],
        out_specs=pl.BlockSpec((tm, D_MODEL), lambda i: (i, 0)),
        compiler_params=_cparams(("parallel",)),
        name="attn_out_proj",
    )(x, o_ctx, o_lat, w, modt)


FCH_TM = 1024


def _fourier_ch_kernel(x_ref, mod_ref, g_ref, cs_ref, a_ref, b_ref):
    h = _mod_rms(x_ref[...], g_ref[...], mod_ref, 3).astype(BF16)
    for gi in range(N_FGROUPS):
        lo = gi * FGROUP_DIM
        r = jnp.dot(h[:, lo:lo + FGROUP_DIM], cs_ref[...], preferred_element_type=F32)
        a_ref[:, lo:lo + FGROUP_DIM] = r[:, :FGROUP_DIM].astype(BF16)
        b_ref[:, lo:lo + FGROUP_DIM] = r[:, FGROUP_DIM:].astype(BF16)


def _fourier_ch(x, modt, l, g, ch_tab):
    tm = FCH_TM
    rows = pl.BlockSpec((tm, D_MODEL), lambda i: (i, 0))
    return pl.pallas_call(
        _fourier_ch_kernel,
        out_shape=(jax.ShapeDtypeStruct((N_TOK, D_MODEL), BF16),
                   jax.ShapeDtypeStruct((N_TOK, D_MODEL), BF16)),
        grid=(N_TOK // tm,),
        in_specs=[
            rows,
            pl.BlockSpec((None, None, N_MOD, D_MODEL), lambda i: (l, _group(i, tm), 0, 0)),
            pl.BlockSpec((1, D_MODEL), lambda i: (0, 0)),
            pl.BlockSpec((FGROUP_DIM, 2 * FGROUP_DIM), lambda i: (0, 0)),
        ],
        out_specs=(rows, rows),
        compiler_params=_cparams(("parallel",)),
        name="fourier_channel_dft",
    )(x, modt, g.reshape(1, D_MODEL), ch_tab)


def _fourier_epilogue(f, x_ref, w_ref, bias_ref, mod_ref):
    mixed = jnp.dot(f.astype(BF16), w_ref[...], preferred_element_type=F32) + bias_ref[...]
    return x_ref[...] + mod_ref[5:6, :] * mixed


def _fourier_ctx_kernel(a_ref, b_ref, c_ref, s_ref, x_ref, w_ref, bias_ref, mod_ref, y_ref):
    f = jnp.dot(c_ref[...], a_ref[...], preferred_element_type=F32)
    f = f + jnp.dot(s_ref[...], b_ref[...], preferred_element_type=F32)
    y_ref[...] = _fourier_epilogue(f, x_ref, w_ref, bias_ref, mod_ref)


def _fourier_ctx(a, b, cos_t, nsin_t, x, w, bias, modt, l):
    rows = lambda dt: pl.BlockSpec((SEQ, D_MODEL), lambda i: (i, 0))
    full = lambda shape: pl.BlockSpec(shape, lambda i: (0, 0))
    return pl.pallas_call(
        _fourier_ctx_kernel,
        out_shape=jax.ShapeDtypeStruct((N_CTX, D_MODEL), F32),
        grid=(BATCH,),
        in_specs=[rows(BF16), rows(BF16), full((SEQ, SEQ)), full((SEQ, SEQ)), rows(F32),
                  full((D_MODEL, D_MODEL)), full((1, D_MODEL)),
                  pl.BlockSpec((None, None, N_MOD, D_MODEL), lambda i: (l, 0, 0, 0))],
        out_specs=rows(F32),
        compiler_params=_cparams(("parallel",)),
        name="fourier_context_positions",
    )(a, b, cos_t, nsin_t, x, w, bias.reshape(1, D_MODEL), modt)


FLAT_TM = 512


def _fourier_lat(a, b, cos_t, nsin_t, x, w, bias, modt, l):
    tm = FLAT_TM
    ni = DEC_SEQ // tm
    r_off = N_CTX // tm
    b_off = N_CTX // DEC_SEQ
    src = pl.BlockSpec((DEC_SEQ, D_MODEL), lambda bb, i: (b_off + bb, 0),
                       pipeline_mode=pl.Buffered(1))
    tab = pl.BlockSpec((tm, DEC_SEQ), lambda bb, i: (i, 0))
    return pl.pallas_call(
        _fourier_ctx_kernel,
        out_shape=jax.ShapeDtypeStruct((N_LAT, D_MODEL), F32),
        grid=(DEC_BATCH, ni),
        in_specs=[src, src, tab, tab,
                  pl.BlockSpec((tm, D_MODEL), lambda bb, i: (r_off + bb * ni + i, 0)),
                  pl.BlockSpec((D_MODEL, D_MODEL), lambda bb, i: (0, 0)),
                  pl.BlockSpec((1, D_MODEL), lambda bb, i: (0, 0)),
                  pl.BlockSpec((None, None, N_MOD, D_MODEL), lambda bb, i: (l, 1 + bb, 0, 0))],
        out_specs=pl.BlockSpec((tm, D_MODEL), lambda bb, i: (bb * ni + i, 0)),
        compiler_params=_cparams(("arbitrary", "arbitrary")),
        name="fourier_latent_positions",
    )(a, b, cos_t, nsin_t, x, w, bias.reshape(1, D_MODEL), modt)


TAB_R = 64
TAB_TM = 256


def _dft_table_kernel(ar_ref, ai_ref, br_ref, bi_ref, cos_ref, nsin_ref):
    br = br_ref[...]
    bi = bi_ref[...]
    for r in range(TAB_TM // TAB_R):
        ar = ar_ref[r]
        ai = ai_ref[r]
        rows = slice(r * TAB_R, (r + 1) * TAB_R)
        cos_ref[rows, :] = (ar * br - ai * bi).astype(BF16)
        nsin_ref[rows, :] = (ai * br + ar * bi).astype(BF16)


def _latent_dft_tables():
    n = DEC_SEQ
    assert n == TAB_R * TAB_R
    idx = jnp.arange(TAB_R, dtype=jnp.int32)
    prod = idx[:, None] * idx[None, :]
    ang_c = (prod % TAB_R).astype(F32) * (2.0 * math.pi / TAB_R)
    ang_f = prod.astype(F32) * (2.0 * math.pi / n)
    cr, ci = jnp.cos(ang_c), -jnp.sin(ang_c)
    scale = 1.0 / math.sqrt(n)
    fr, fi = jnp.cos(ang_f) * scale, jnp.sin(ang_f) * (-scale)
    ar = jnp.tile(cr, (1, TAB_R)).reshape(TAB_R, 1, n)
    ai = jnp.tile(ci, (1, TAB_R)).reshape(TAB_R, 1, n)
    br = (cr[:, :, None] * fr[:, None, :] - ci[:, :, None] * fi[:, None, :]).reshape(TAB_R, n)
    bi = (cr[:, :, None] * fi[:, None, :] + ci[:, :, None] * fr[:, None, :]).reshape(TAB_R, n)
    per = TAB_TM // TAB_R
    coarse = pl.BlockSpec((per, 1, n), lambda i: (i, 0, 0))
    fine = pl.BlockSpec((TAB_R, n), lambda i: (0, 0))
    out = pl.BlockSpec((TAB_TM, n), lambda i: (i, 0))
    return pl.pallas_call(
        _dft_table_kernel,
        out_shape=(jax.ShapeDtypeStruct((n, n), BF16), jax.ShapeDtypeStruct((n, n), BF16)),
        grid=(n // TAB_TM,),
        in_specs=[coarse, coarse, fine, fine],
        out_specs=(out, out),
        compiler_params=_cparams(("parallel",)),
        name="latent_dft_tables",
    )(ar, ai, br, bi)


def _dft_tables(n, scale):
    idx = jnp.arange(n, dtype=jnp.int32)
    ang = ((idx[:, None] * idx[None, :]) % n).astype(F32) * (2.0 * math.pi / n)
    return (jnp.cos(ang) * scale).astype(BF16), (jnp.sin(ang) * (-scale)).astype(BF16)


def _rope_tables(tm):
    rows = DEC_SEQ // GRID_W
    row = jnp.repeat(jnp.arange(rows, dtype=F32), GRID_W)
    col = jnp.tile(jnp.arange(GRID_W, dtype=F32), rows)
    inv = 1.0 / (ROPE_THETA ** (jnp.arange(AXIS_FREQS, dtype=F32) / AXIS_FREQS))
    ar = row[:, None] * inv
    ac = col[:, None] * inv
    pad = jnp.zeros((DEC_SEQ, ROPE_PAD - QK_ROPE), F32)
    cos = jnp.concatenate([jnp.cos(ar), jnp.cos(ar), jnp.cos(ac), jnp.cos(ac), pad + 1.0], axis=1)
    sin = jnp.concatenate([-jnp.sin(ar), jnp.sin(ar), -jnp.sin(ac), jnp.sin(ac), pad], axis=1)
    cos = jnp.concatenate([cos, jnp.ones((tm, ROPE_PAD), F32)], axis=0)
    sin = jnp.concatenate([sin, jnp.zeros((tm, ROPE_PAD), F32)], axis=0)
    return cos, sin


def _pad_heads(w, part, width):
    k = w.shape[0]
    w = w.reshape(k, N_HEADS, -1)[:, :, part]
    return jnp.pad(w, ((0, 0), (0, 0), (0, width - w.shape[-1])))


def kernel(x_prompt, x_sample, cache_ckv, cache_kpe, c, c_ctx, w_mod, b_mod, norm_g, ffn_wg, ffn_wu,
           ffn_wd, mla_w_dq, mla_q_norm, mla_w_uq, mla_w_dkv, mla_kv_norm, mla_w_ukv, mla_w_o,
           fourier_w, fourier_b, final_norm):
    xs = (x_prompt.reshape(N_CTX, D_MODEL), x_sample.reshape(N_LAT, D_MODEL))
    cond = jnp.concatenate([c_ctx[None, :], c, jnp.zeros((N_GROUPS - 1 - DEC_BATCH, D_MODEL), F32)], axis=0)
    modt = _modulation(cond, w_mod, b_mod)

    wg = ffn_wg.astype(BF16)
    wu = ffn_wu.astype(BF16)
    wd = ffn_wd.astype(BF16)

    rope_cos, rope_sin = _rope_tables(MLA_TM)
    ch_cos, ch_nsin = _dft_tables(FGROUP_DIM, 1.0 / math.sqrt(FGROUP_DIM))
    ch_tab = jnp.concatenate([ch_cos, -ch_nsin], axis=1)
    ctx_cos, ctx_nsin = _dft_tables(SEQ, 1.0 / math.sqrt(SEQ))
    lat_cos, lat_nsin = _latent_dft_tables()

    ckv_states = []
    kpe_states = []
    for l in range(DEPTH):
        j = l // 2
        x = _ffn(xs if l == 0 else (x,), modt, l, 0, norm_g[l, 0], wg, wu, wd, final_norm, False)
        if l % 2 == 0:
            wuq = mla_w_uq[j]
            wuq_p = jnp.concatenate(
                [_pad_heads(wuq, slice(0, QK_NOPE), QK_NOPE),
                 _pad_heads(wuq, slice(QK_NOPE, QK_NOPE + QK_ROPE), ROPE_PAD)], axis=-1)
            wuq_p = wuq_p.reshape(Q_LORA, N_HEADS * HEAD_PAD).astype(BF16)
            wdkv_p = jnp.pad(mla_w_dkv[j], ((0, 0), (0, ROPE_PAD - QK_ROPE))).astype(BF16)
            wukv = mla_w_ukv[j]
            wuk = _pad_heads(wukv, slice(0, QK_NOPE), QK_NOPE).reshape(KV_LORA, -1).astype(BF16)
            wuvt = _pad_heads(wukv, slice(QK_NOPE, QK_NOPE + V_DIM), V_DIM).reshape(KV_LORA, -1).T.astype(BF16)
            q, k, vt, new_ckv, new_kpe = _mla_proj(
                x, modt, l, norm_g[l, 1], mla_w_dq[j].astype(BF16), mla_q_norm[j], wuq_p, wdkv_p,
                mla_kv_norm[j], wuk, wuvt, rope_cos, rope_sin)
            ckv_states.append(new_ckv)
            kpe_states.append(new_kpe)
            kc, vct = _cache_kv(
                cache_ckv[:, j].reshape(DEC_BATCH * PAST_LEN, KV_LORA),
                jnp.pad(cache_kpe[:, j].reshape(DEC_BATCH * PAST_LEN, QK_ROPE),
                        ((0, 0), (0, ROPE_PAD - QK_ROPE))),
                wuk, wuvt)
            o_ctx = _ctx_attn(q, k, vt)
            o_lat = _lat_attn(q, k, vt, kc, vct)
            mixed = (_oproj(x, o_ctx, o_lat, mla_w_o[j].astype(BF16), modt, l),)
        else:
            a, b = _fourier_ch(x, modt, l, norm_g[l, 1], ch_tab)
            fw = fourier_w[j].astype(BF16)
            mixed = (_fourier_ctx(a, b, ctx_cos, ctx_nsin, x, fw, fourier_b[j], modt, l),
                     _fourier_lat(a, b, lat_cos, lat_nsin, x, fw, fourier_b[j], modt, l))
        x = _ffn(mixed, modt, l, 1, norm_g[l, 2], wg, wu, wd, final_norm, l == DEPTH - 1)

    y_ctx, y_lat = x
    return (y_ctx.reshape(BATCH, SEQ, D_MODEL), y_lat.reshape(DEC_BATCH, DEC_SEQ, D_MODEL),
            jnp.stack(ckv_states, axis=1), jnp.stack(kpe_states, axis=1))
```

```python
import functools
import math

import jax
import jax.numpy as jnp
from jax import lax
from jax.experimental import pallas as pl
from jax.experimental.pallas import tpu as pltpu

F32 = jnp.float32
BF16 = jnp.bfloat16

D_MODEL = 1024
BATCH = 32
SEQ = 256
DEPTH = 4
DEC_BATCH = 2
DEC_SEQ = 4096
PAST_LEN = 512
GRID_W = 64
N_HEADS = 8
QK_NOPE = 128
QK_ROPE = 64
V_DIM = 128
Q_LORA = 512
KV_LORA = 256
AXIS_FREQS = QK_ROPE // 4
ROPE_THETA = 10000.0
N_FGROUPS = 4
FGROUP_DIM = D_MODEL // N_FGROUPS
D_FF = 2816
N_MOD = 9
EPS = 1e-6
ATTN_SCALE = 1.0 / math.sqrt(QK_NOPE + QK_ROPE)
Q_SCALE = ATTN_SCALE * math.log2(math.e)

N_CTX = BATCH * SEQ
N_LAT = DEC_BATCH * DEC_SEQ
N_TOK = N_CTX + N_LAT
N_GROUPS = 8
HEAD_PAD = 256
ROPE_PAD = 128
VMEM_LIMIT = 56 * 1024 * 1024


def _cparams(sem):
    return pltpu.CompilerParams(dimension_semantics=sem, vmem_limit_bytes=VMEM_LIMIT)


def _group(i, tm):
    n_ctx = N_CTX // tm
    per_batch = DEC_SEQ // tm
    return jnp.where(i < n_ctx, 0, 1 + (i - n_ctx) // per_batch)


def _rms(x, g):
    return x * lax.rsqrt(jnp.mean(x * x, axis=-1, keepdims=True) + EPS) * g


def _mod_rms(x, g, mod, k):
    gain = g * (1.0 + mod[k + 1:k + 2, :])
    return x * lax.rsqrt(jnp.mean(x * x, axis=-1, keepdims=True) + EPS) * gain + mod[k:k + 1, :]


def _mod_kernel(c_ref, w_ref, b_ref, o_ref):
    c = c_ref[...]
    s = (c * jax.nn.sigmoid(c)).astype(BF16)
    o_ref[...] = jnp.dot(s, w_ref[...].astype(BF16), preferred_element_type=F32) + b_ref[...]


def _modulation(cond, w_mod, b_mod):
    out = pl.pallas_call(
        _mod_kernel,
        out_shape=jax.ShapeDtypeStruct((DEPTH, N_MOD, N_GROUPS, D_MODEL), F32),
        grid=(DEPTH, N_MOD),
        in_specs=[
            pl.BlockSpec((N_GROUPS, D_MODEL), lambda l, k: (0, 0)),
            pl.BlockSpec((None, D_MODEL, D_MODEL), lambda l, k: (l, 0, k)),
            pl.BlockSpec((None, None, 1, D_MODEL), lambda l, k: (l, k, 0, 0)),
        ],
        out_specs=pl.BlockSpec((None, None, N_GROUPS, D_MODEL), lambda l, k: (l, k, 0, 0)),
        compiler_params=_cparams(("arbitrary", "arbitrary")),
        name="adaln_modulation",
    )(cond, w_mod, b_mod.reshape(DEPTH, N_MOD, 1, D_MODEL))
    return out.transpose(0, 2, 1, 3)


FFN_TM = 512
FFN_TF = 256
FFN_SUB = 256
FFN_CTX_TILES = N_CTX // FFN_TM


def _ffn_kernel(*refs, k0, split_in, final):
    n_x = 2 if split_in else 1
    x_refs, (mod_ref, g_ref, wg_ref, wu_ref, wd_ref, fn_ref) = refs[:n_x], refs[n_x:n_x + 6]
    n_out = 2 if final else 1
    out_refs, (a_ref,) = refs[n_x + 6:n_x + 6 + n_out], refs[n_x + 6 + n_out:]
    is_ctx = pl.program_id(0) < FFN_CTX_TILES

    ys = []
    for r in range(FFN_TM // FFN_SUB):
        rows = slice(r * FFN_SUB, (r + 1) * FFN_SUB)
        if split_in:
            x = jnp.where(is_ctx, x_refs[0][rows, :], x_refs[1][rows, :])
        else:
            x = x_refs[0][rows, :]
        h = _mod_rms(x, g_ref[...], mod_ref, k0).astype(BF16)
        for c in range(D_FF // FFN_TF):
            cols = slice(c * FFN_TF, (c + 1) * FFN_TF)
            g = jnp.dot(h, wg_ref[:, cols], preferred_element_type=F32)
            u = jnp.dot(h, wu_ref[:, cols], preferred_element_type=F32)
            a_ref[rows, cols] = (g * jax.nn.sigmoid(g) * u).astype(BF16)
        y = jnp.dot(a_ref[rows, :], wd_ref[...], preferred_element_type=F32)
        y = x + (0.5 * mod_ref[k0 + 2:k0 + 3, :]) * y
        if final:
            ys.append((rows, _rms(y, fn_ref[...])))
        else:
            out_refs[0][rows, :] = y

    if final:
        @pl.when(is_ctx)
        def _():
            for rows, y in ys:
                out_refs[0][rows, :] = y

        @pl.when(jnp.logical_not(is_ctx))
        def _():
            for rows, y in ys:
                out_refs[1][rows, :] = y


def _ffn(xs, modt, l, half, g, wg, wu, wd, final_norm, final):
    tm = FFN_TM
    k0 = 6 * half
    split_in = len(xs) == 2
    rows = pl.BlockSpec((tm, D_MODEL), lambda i: (i, 0))
    ctx_rows = pl.BlockSpec((tm, D_MODEL), lambda i: (jnp.minimum(i, FFN_CTX_TILES - 1), 0))
    lat_rows = pl.BlockSpec((tm, D_MODEL), lambda i: (jnp.maximum(i - FFN_CTX_TILES, 0), 0))
    resident = lambda shape: pl.BlockSpec((None, None) + shape, lambda i: (l, half, 0, 0),
                                          pipeline_mode=pl.Buffered(1))
    if final:
        out_shape = (jax.ShapeDtypeStruct((N_CTX, D_MODEL), F32),
                     jax.ShapeDtypeStruct((N_LAT, D_MODEL), F32))
        out_specs = (ctx_rows, lat_rows)
    else:
        out_shape = jax.ShapeDtypeStruct((N_TOK, D_MODEL), F32)
        out_specs = rows
    return pl.pallas_call(
        functools.partial(_ffn_kernel, k0=k0, split_in=split_in, final=final),
        out_shape=out_shape,
        grid=(N_TOK // tm,),
        in_specs=([ctx_rows, lat_rows] if split_in else [rows]) + [
            pl.BlockSpec((None, None, N_MOD, D_MODEL), lambda i: (l, _group(i, tm), 0, 0)),
            pl.BlockSpec((1, D_MODEL), lambda i: (0, 0)),
            resident((D_MODEL, D_FF)),
            resident((D_MODEL, D_FF)),
            resident((D_FF, D_MODEL)),
            pl.BlockSpec((1, D_MODEL), lambda i: (0, 0)),
        ],
        out_specs=out_specs,
        scratch_shapes=[pltpu.VMEM((tm, D_FF), BF16)],
        compiler_params=_cparams(("arbitrary",)),
        name="swiglu_halfstep",
    )(*xs, modt, g.reshape(1, D_MODEL), wg, wu, wd, final_norm.reshape(1, D_MODEL))


MLA_TM = 512


def _swap_halves(x):
    lane = lax.broadcasted_iota(jnp.int32, x.shape, 1)
    fwd = pltpu.roll(x, ROPE_PAD - AXIS_FREQS, axis=1)
    bwd = pltpu.roll(x, AXIS_FREQS, axis=1)
    return jnp.where(lane % (2 * AXIS_FREQS) < AXIS_FREQS, fwd, bwd)


def _rope(x, cos, sin):
    return x * cos + _swap_halves(x) * sin


MLA_SUB = SEQ


def _kv_up(ckv, kpe_pad, wuk_ref, wuvt_ref, k_ref, vt_ref, rows):
    c = ckv.astype(BF16)
    kn = jnp.dot(c, wuk_ref[...], preferred_element_type=F32).astype(BF16)
    vt = lax.dot_general(wuvt_ref[...], c, _NT, preferred_element_type=F32)
    vt_ref[:, rows] = vt.astype(BF16)
    kp = kpe_pad.astype(BF16)
    for hd in range(N_HEADS):
        k_ref[rows, hd * HEAD_PAD:hd * HEAD_PAD + QK_NOPE] = kn[:, hd * QK_NOPE:(hd + 1) * QK_NOPE]
        k_ref[rows, hd * HEAD_PAD + QK_NOPE:(hd + 1) * HEAD_PAD] = kp


def _mla_proj_kernel(x_ref, mod_ref, g_ref, wdq_ref, qn_ref, wuq_ref, wdkv_ref, kvn_ref, wuk_ref,
                     wuvt_ref, cos_ref, sin_ref, q_ref, k_ref, vt_ref, ckv_ref, kpe_ref):
    states = []
    for r in range(MLA_TM // MLA_SUB):
        rows = slice(r * MLA_SUB, (r + 1) * MLA_SUB)
        h = _mod_rms(x_ref[rows, :], g_ref[...], mod_ref, 3).astype(BF16)
        cq = _rms(jnp.dot(h, wdq_ref[...], preferred_element_type=F32), qn_ref[...]).astype(BF16)
        q = jnp.dot(cq, wuq_ref[...], preferred_element_type=F32)
        cos = cos_ref[rows, :]
        sin = sin_ref[rows, :]
        for hd in range(N_HEADS):
            lo = hd * HEAD_PAD
            q_ref[rows, lo:lo + QK_NOPE] = (q[:, lo:lo + QK_NOPE] * Q_SCALE).astype(BF16)
            qr = _rope(q[:, lo + QK_NOPE:lo + HEAD_PAD], cos, sin)
            q_ref[rows, lo + QK_NOPE:lo + HEAD_PAD] = (qr * Q_SCALE).astype(BF16)
        kv = jnp.dot(h, wdkv_ref[...], preferred_element_type=F32)
        ckv = _rms(kv[:, :KV_LORA], kvn_ref[...])
        kpe = _rope(kv[:, KV_LORA:], cos, sin)
        _kv_up(ckv, kpe, wuk_ref, wuvt_ref, k_ref, vt_ref, rows)
        states.append((ckv, kpe[:, :QK_ROPE]))

    @pl.when(pl.program_id(0) < N_CTX // MLA_TM)
    def _():
        for r, (ckv, kpe) in enumerate(states):
            ckv_ref[r] = ckv
            kpe_ref[r] = kpe


def _mla_proj(x, modt, l, g, wdq, qn, wuq, wdkv, kvn, wuk, wuvt, cos_t, sin_t):
    tm = MLA_TM
    n_ctx = N_CTX // tm
    per_batch = DEC_SEQ // tm

    def rope_idx(i):
        return jnp.where(i < n_ctx, per_batch, (i - n_ctx) % per_batch)

    full = lambda shape: pl.BlockSpec(shape, lambda i: (0, 0))
    rows = lambda w: pl.BlockSpec((tm, w), lambda i: (i, 0))
    state = lambda w: pl.BlockSpec((tm // SEQ, SEQ, w),
                                   lambda i: (jnp.minimum(i, n_ctx - 1), 0, 0))
    return pl.pallas_call(
        _mla_proj_kernel,
        out_shape=(
            jax.ShapeDtypeStruct((N_TOK, N_HEADS * HEAD_PAD), BF16),
            jax.ShapeDtypeStruct((N_TOK, N_HEADS * HEAD_PAD), BF16),
            jax.ShapeDtypeStruct((N_HEADS * V_DIM, N_TOK), BF16),
            jax.ShapeDtypeStruct((BATCH, SEQ, KV_LORA), F32),
            jax.ShapeDtypeStruct((BATCH, SEQ, QK_ROPE), F32),
        ),
        grid=(N_TOK // tm,),
        in_specs=[
            rows(D_MODEL),
            pl.BlockSpec((None, None, N_MOD, D_MODEL), lambda i: (l, _group(i, tm), 0, 0)),
            full((1, D_MODEL)),
            full((D_MODEL, Q_LORA)),
            full((1, Q_LORA)),
            full((Q_LORA, N_HEADS * HEAD_PAD)),
            full((D_MODEL, KV_LORA + ROPE_PAD)),
            full((1, KV_LORA)),
            full((KV_LORA, N_HEADS * QK_NOPE)),
            full((N_HEADS * V_DIM, KV_LORA)),
            pl.BlockSpec((tm, ROPE_PAD), lambda i: (rope_idx(i), 0)),
            pl.BlockSpec((tm, ROPE_PAD), lambda i: (rope_idx(i), 0)),
        ],
        out_specs=(rows(N_HEADS * HEAD_PAD), rows(N_HEADS * HEAD_PAD),
                   pl.BlockSpec((N_HEADS * V_DIM, tm), lambda i: (0, i)),
                   state(KV_LORA), state(QK_ROPE)),
        compiler_params=_cparams(("arbitrary",)),
        name="mla_projections",
    )(x, modt, g.reshape(1, D_MODEL), wdq, qn.reshape(1, Q_LORA), wuq, wdkv,
      kvn.reshape(1, KV_LORA), wuk, wuvt, cos_t, sin_t)


def _cache_kv_kernel(ckv_ref, kpe_ref, wuk_ref, wuvt_ref, k_ref, vt_ref):
    _kv_up(ckv_ref[...], kpe_ref[...], wuk_ref, wuvt_ref, k_ref, vt_ref, slice(None))


def _cache_kv(ckv, kpe_pad, wuk, wuvt):
    n = DEC_BATCH * PAST_LEN
    tm = PAST_LEN
    full = lambda shape: pl.BlockSpec(shape, lambda i: (0, 0))
    rows = lambda w: pl.BlockSpec((tm, w), lambda i: (i, 0))
    return pl.pallas_call(
        _cache_kv_kernel,
        out_shape=(jax.ShapeDtypeStruct((n, N_HEADS * HEAD_PAD), BF16),
                   jax.ShapeDtypeStruct((N_HEADS * V_DIM, n), BF16)),
        grid=(n // tm,),
        in_specs=[rows(KV_LORA), rows(ROPE_PAD),
                  full((KV_LORA, N_HEADS * QK_NOPE)), full((N_HEADS * V_DIM, KV_LORA))],
        out_specs=(rows(N_HEADS * HEAD_PAD), pl.BlockSpec((N_HEADS * V_DIM, tm), lambda i: (0, i))),
        compiler_params=_cparams(("parallel",)),
        name="cache_kv_up",
    )(ckv, kpe_pad, wuk, wuvt)


_NT = (((1,), (1,)), ((), ()))


def _ctx_attn_kernel(q_ref, k_ref, vt_ref, o_ref):
    for hd in range(N_HEADS):
        q = q_ref[:, hd * HEAD_PAD:(hd + 1) * HEAD_PAD]
        k = k_ref[:, hd * HEAD_PAD:(hd + 1) * HEAD_PAD]
        st = lax.dot_general(k, q, _NT, preferred_element_type=F32)
        p = jnp.exp2(st - jnp.max(st, axis=0, keepdims=True))
        ot = jnp.dot(vt_ref[hd * V_DIM:(hd + 1) * V_DIM, :], p.astype(BF16),
                     preferred_element_type=F32)
        ot = ot / jnp.sum(p, axis=0, keepdims=True)
        o_ref[:, hd * V_DIM:(hd + 1) * V_DIM] = ot.T.astype(BF16)


def _ctx_attn(q, k, vt):
    rows = lambda w: pl.BlockSpec((SEQ, w), lambda b: (b, 0))
    return pl.pallas_call(
        _ctx_attn_kernel,
        out_shape=jax.ShapeDtypeStruct((N_CTX, N_HEADS * V_DIM), BF16),
        grid=(BATCH,),
        in_specs=[rows(N_HEADS * HEAD_PAD), rows(N_HEADS * HEAD_PAD),
                  pl.BlockSpec((N_HEADS * V_DIM, SEQ), lambda b: (0, b))],
        out_specs=rows(N_HEADS * V_DIM),
        compiler_params=_cparams(("parallel",)),
        name="context_attention",
    )(q, k, vt)


LAT_TQ = 256
LAT_KC = 512
LAT_KEYS = PAST_LEN + DEC_SEQ
SUBLANES = 8


def _lat_attn_kernel(q_ref, kc_ref, kl_ref, vct_ref, vlt_ref, o_ref, s_ref, m_ref):
    tq = q_ref.shape[0]
    kc = LAT_KC
    n_cache = PAST_LEN // kc
    n_chunks = LAT_KEYS // kc
    step = pl.program_id(2)

    @pl.when((pl.program_id(0) == 0) & (pl.program_id(1) == 0) & (step == 0))
    def _():
        s_ref[1] = jnp.zeros(s_ref.shape[1:], F32)
        m_ref[1] = jnp.zeros(m_ref.shape[1:], F32)

    def rows8(x, op):
        return op(x.reshape(kc // SUBLANES, SUBLANES, tq), axis=0)

    def body(cur):
        prev = 1 - cur
        q = q_ref[...]
        m = jnp.max(m_ref[prev], axis=0, keepdims=True)
        mpart = None
        lpart = jnp.zeros((SUBLANES, tq), F32)
        acc = jnp.zeros((V_DIM, tq), F32)
        for c in range(n_chunks):
            rows = slice(c * kc, (c + 1) * kc)
            if c < n_cache:
                k = kc_ref[rows, :]
                vt = vct_ref[:, rows]
            else:
                lat = slice((c - n_cache) * kc, (c - n_cache + 1) * kc)
                k = kl_ref[lat, :]
                vt = vlt_ref[:, lat]
            st = lax.dot_general(k, q, _NT, preferred_element_type=F32)
            s_ref[cur, rows, :] = st
            cm = rows8(st, jnp.max)
            mpart = cm if mpart is None else jnp.maximum(mpart, cm)
            p = jnp.exp2(s_ref[prev, rows, :] - m)
            lpart = lpart + rows8(p, jnp.sum)
            acc = acc + jnp.dot(vt, p.astype(BF16), preferred_element_type=F32)
        m_ref[cur] = mpart
        denom = jnp.sum(lpart, axis=0, keepdims=True)
        o_ref[...] = (acc / denom).T.astype(BF16)

    for parity in range(2):
        pl.when(step % 2 == parity)(functools.partial(body, parity))


def _lat_attn(q, k, vt, kc, vct):
    tq = LAT_TQ
    q_off = N_CTX // tq
    l_off = N_CTX // DEC_SEQ
    nq = DEC_SEQ // tq
    return pl.pallas_call(
        _lat_attn_kernel,
        out_shape=jax.ShapeDtypeStruct((N_LAT, N_HEADS * V_DIM), BF16),
        grid=(DEC_BATCH, N_HEADS, nq + 1),
        in_specs=[
            pl.BlockSpec((tq, HEAD_PAD),
                         lambda b, h, i: (q_off + b * nq + jnp.minimum(i, nq - 1), h)),
            pl.BlockSpec((PAST_LEN, HEAD_PAD), lambda b, h, i: (b, h)),
            pl.BlockSpec((DEC_SEQ, HEAD_PAD), lambda b, h, i: (l_off + b, h)),
            pl.BlockSpec((V_DIM, PAST_LEN), lambda b, h, i: (h, b)),
            pl.BlockSpec((V_DIM, DEC_SEQ), lambda b, h, i: (h, l_off + b)),
        ],
        out_specs=pl.BlockSpec((tq, V_DIM),
                               lambda b, h, i: (b * nq + jnp.maximum(i - 1, 0), h)),
        scratch_shapes=[pltpu.VMEM((2, LAT_KEYS, tq), F32), pltpu.VMEM((2, SUBLANES, tq), F32)],
        compiler_params=_cparams(("arbitrary", "arbitrary", "arbitrary")),
        name="latent_attention",
    )(q, kc, k, vct, vt)


OPROJ_TM = 1024


def _oproj_kernel(x_ref, oc_ref, ol_ref, w_ref, mod_ref, y_ref):
    o = jnp.where(pl.program_id(0) < N_CTX // OPROJ_TM, oc_ref[...], ol_ref[...])
    y_ref[...] = x_ref[...] + mod_ref[5:6, :] * jnp.dot(o, w_ref[...], preferred_element_type=F32)


def _oproj(x, o_ctx, o_lat, w, modt, l):
    tm = OPROJ_TM
    n_ctx = N_CTX // tm
    return pl.pallas_call(
        _oproj_kernel,
        out_shape=jax.ShapeDtypeStruct((N_TOK, D_MODEL), F32),
        grid=(N_TOK // tm,),
        in_specs=[
            pl.BlockSpec((tm, D_MODEL), lambda i: (i, 0)),
            pl.BlockSpec((tm, N_HEADS * V_DIM), lambda i: (jnp.minimum(i, n_ctx - 1), 0)),
            pl.BlockSpec((tm, N_HEADS * V_DIM), lambda i: (jnp.maximum(i - n_ctx, 0), 0)),
            pl.BlockSpec((N_HEADS * V_DIM, D_MODEL), lambda i: (0, 0)),
            pl.BlockSpec((None, None, N_MOD, D_MODEL), lambda i: (l, _group(i, tm), 0, 0)),
        ],
        out_specs=pl.BlockSpec((tm, D_MODEL), lambda i: (i, 0)),
        compiler_params=_cparams(("parallel",)),
        name="attn_out_proj",
    )(x, o_ctx, o_lat, w, modt)


FCH_TM = 1024


def _fourier_ch_kernel(x_ref, mod_ref, g_ref, cs_ref, a_ref, b_ref):
    h = _mod_rms(x_ref[...], g_ref[...], mod_ref, 3).astype(BF16)
    for gi in range(N_FGROUPS):
        lo = gi * FGROUP_DIM
        r = jnp.dot(h[:, lo:lo + FGROUP_DIM], cs_ref[...], preferred_element_type=F32)
        a_ref[:, lo:lo + FGROUP_DIM] = r[:, :FGROUP_DIM].astype(BF16)
        b_ref[:, lo:lo + FGROUP_DIM] = r[:, FGROUP_DIM:].astype(BF16)


def _fourier_ch(x, modt, l, g, ch_tab):
    tm = FCH_TM
    rows = pl.BlockSpec((tm, D_MODEL), lambda i: (i, 0))
    return pl.pallas_call(
        _fourier_ch_kernel,
        out_shape=(jax.ShapeDtypeStruct((N_TOK, D_MODEL), BF16),
                   jax.ShapeDtypeStruct((N_TOK, D_MODEL), BF16)),
        grid=(N_TOK // tm,),
        in_specs=[
            rows,
            pl.BlockSpec((None, None, N_MOD, D_MODEL), lambda i: (l, _group(i, tm), 0, 0)),
            pl.BlockSpec((1, D_MODEL), lambda i: (0, 0)),
            pl.BlockSpec((FGROUP_DIM, 2 * FGROUP_DIM), lambda i: (0, 0)),
        ],
        out_specs=(rows, rows),
        compiler_params=_cparams(("parallel",)),
        name="fourier_channel_dft",
    )(x, modt, g.reshape(1, D_MODEL), ch_tab)


def _fourier_epilogue(f, x_ref, w_ref, bias_ref, mod_ref):
    mixed = jnp.dot(f.astype(BF16), w_ref[...], preferred_element_type=F32) + bias_ref[...]
    return x_ref[...] + mod_ref[5:6, :] * mixed


def _fourier_ctx_kernel(a_ref, b_ref, c_ref, s_ref, x_ref, w_ref, bias_ref, mod_ref, y_ref):
    f = jnp.dot(c_ref[...], a_ref[...], preferred_element_type=F32)
    f = f + jnp.dot(s_ref[...], b_ref[...], preferred_element_type=F32)
    y_ref[...] = _fourier_epilogue(f, x_ref, w_ref, bias_ref, mod_ref)


def _fourier_ctx(a, b, cos_t, nsin_t, x, w, bias, modt, l):
    rows = lambda dt: pl.BlockSpec((SEQ, D_MODEL), lambda i: (i, 0))
    full = lambda shape: pl.BlockSpec(shape, lambda i: (0, 0))
    return pl.pallas_call(
        _fourier_ctx_kernel,
        out_shape=jax.ShapeDtypeStruct((N_CTX, D_MODEL), F32),
        grid=(BATCH,),
        in_specs=[rows(BF16), rows(BF16), full((SEQ, SEQ)), full((SEQ, SEQ)), rows(F32),
                  full((D_MODEL, D_MODEL)), full((1, D_MODEL)),
                  pl.BlockSpec((None, None, N_MOD, D_MODEL), lambda i: (l, 0, 0, 0))],
        out_specs=rows(F32),
        compiler_params=_cparams(("parallel",)),
        name="fourier_context_positions",
    )(a, b, cos_t, nsin_t, x, w, bias.reshape(1, D_MODEL), modt)


FLAT_TM = 512


def _fourier_lat(a, b, cos_t, nsin_t, x, w, bias, modt, l):
    tm = FLAT_TM
    ni = DEC_SEQ // tm
    r_off = N_CTX // tm
    b_off = N_CTX // DEC_SEQ
    src = pl.BlockSpec((DEC_SEQ, D_MODEL), lambda bb, i: (b_off + bb, 0),
                       pipeline_mode=pl.Buffered(1))
    tab = pl.BlockSpec((tm, DEC_SEQ), lambda bb, i: (i, 0))
    return pl.pallas_call(
        _fourier_ctx_kernel,
        out_shape=jax.ShapeDtypeStruct((N_LAT, D_MODEL), F32),
        grid=(DEC_BATCH, ni),
        in_specs=[src, src, tab, tab,
                  pl.BlockSpec((tm, D_MODEL), lambda bb, i: (r_off + bb * ni + i, 0)),
                  pl.BlockSpec((D_MODEL, D_MODEL), lambda bb, i: (0, 0)),
                  pl.BlockSpec((1, D_MODEL), lambda bb, i: (0, 0)),
                  pl.BlockSpec((None, None, N_MOD, D_MODEL), lambda bb, i: (l, 1 + bb, 0, 0))],
        out_specs=pl.BlockSpec((tm, D_MODEL), lambda bb, i: (bb * ni + i, 0)),
        compiler_params=_cparams(("arbitrary", "arbitrary")),
        name="fourier_latent_positions",
    )(a, b, cos_t, nsin_t, x, w, bias.reshape(1, D_MODEL), modt)


TAB_R = 64
TAB_TM = 256


def _dft_table_kernel(ar_ref, ai_ref, br_ref, bi_ref, cos_ref, nsin_ref):
    br = br_ref[...]
    bi = bi_ref[...]
    for r in range(TAB_TM // TAB_R):
        ar = ar_ref[r]
        ai = ai_ref[r]
        rows = slice(r * TAB_R, (r + 1) * TAB_R)
        cos_ref[rows, :] = (ar * br - ai * bi).astype(BF16)
        nsin_ref[rows, :] = (ai * br + ar * bi).astype(BF16)


def _latent_dft_tables():
    n = DEC_SEQ
    assert n == TAB_R * TAB_R
    idx = jnp.arange(TAB_R, dtype=jnp.int32)
    prod = idx[:, None] * idx[None, :]
    ang_c = (prod % TAB_R).astype(F32) * (2.0 * math.pi / TAB_R)
    ang_f = prod.astype(F32) * (2.0 * math.pi / n)
    cr, ci = jnp.cos(ang_c), -jnp.sin(ang_c)
    scale = 1.0 / math.sqrt(n)
    fr, fi = jnp.cos(ang_f) * scale, jnp.sin(ang_f) * (-scale)
    ar = jnp.tile(cr, (1, TAB_R)).reshape(TAB_R, 1, n)
    ai = jnp.tile(ci, (1, TAB_R)).reshape(TAB_R, 1, n)
    br = (cr[:, :, None] * fr[:, None, :] - ci[:, :, None] * fi[:, None, :]).reshape(TAB_R, n)
    bi = (cr[:, :, None] * fi[:, None, :] + ci[:, :, None] * fr[:, None, :]).reshape(TAB_R, n)
    per = TAB_TM // TAB_R
    coarse = pl.BlockSpec((per, 1, n), lambda i: (i, 0, 0))
    fine = pl.BlockSpec((TAB_R, n), lambda i: (0, 0))
    out = pl.BlockSpec((TAB_TM, n), lambda i: (i, 0))
    return pl.pallas_call(
        _dft_table_kernel,
        out_shape=(jax.ShapeDtypeStruct((n, n), BF16), jax.ShapeDtypeStruct((n, n), BF16)),
        grid=(n // TAB_TM,),
        in_specs=[coarse, coarse, fine, fine],
        out_specs=(out, out),
        compiler_params=_cparams(("parallel",)),
        name="latent_dft_tables",
    )(ar, ai, br, bi)


def _dft_tables(n, scale):
    idx = jnp.arange(n, dtype=jnp.int32)
    ang = ((idx[:, None] * idx[None, :]) % n).astype(F32) * (2.0 * math.pi / n)
    return (jnp.cos(ang) * scale).astype(BF16), (jnp.sin(ang) * (-scale)).astype(BF16)


def _rope_tables(tm):
    rows = DEC_SEQ // GRID_W
    row = jnp.repeat(jnp.arange(rows, dtype=F32), GRID_W)
    col = jnp.tile(jnp.arange(GRID_W, dtype=F32), rows)
    inv = 1.0 / (ROPE_THETA ** (jnp.arange(AXIS_FREQS, dtype=F32) / AXIS_FREQS))
    ar = row[:, None] * inv
    ac = col[:, None] * inv
    pad = jnp.zeros((DEC_SEQ, ROPE_PAD - QK_ROPE), F32)
    cos = jnp.concatenate([jnp.cos(ar), jnp.cos(ar), jnp.cos(ac), jnp.cos(ac), pad + 1.0], axis=1)
    sin = jnp.concatenate([-jnp.sin(ar), jnp.sin(ar), -jnp.sin(ac), jnp.sin(ac), pad], axis=1)
    cos = jnp.concatenate([cos, jnp.ones((tm, ROPE_PAD), F32)], axis=0)
    sin = jnp.concatenate([sin, jnp.zeros((tm, ROPE_PAD), F32)], axis=0)
    return cos, sin


def _pad_heads(w, part, width):
    k = w.shape[0]
    w = w.reshape(k, N_HEADS, -1)[:, :, part]
    return jnp.pad(w, ((0, 0), (0, 0), (0, width - w.shape[-1])))


def kernel(x_prompt, x_sample, cache_ckv, cache_kpe, c, c_ctx, w_mod, b_mod, norm_g, ffn_wg, ffn_wu,
           ffn_wd, mla_w_dq, mla_q_norm, mla_w_uq, mla_w_dkv, mla_kv_norm, mla_w_ukv, mla_w_o,
           fourier_w, fourier_b, final_norm):
    xs = (x_prompt.reshape(N_CTX, D_MODEL), x_sample.reshape(N_LAT, D_MODEL))
    cond = jnp.concatenate([c_ctx[None, :], c, jnp.zeros((N_GROUPS - 1 - DEC_BATCH, D_MODEL), F32)], axis=0)
    modt = _modulation(cond, w_mod, b_mod)

    wg = ffn_wg.astype(BF16)
    wu = ffn_wu.astype(BF16)
    wd = ffn_wd.astype(BF16)

    rope_cos, rope_sin = _rope_tables(MLA_TM)
    ch_cos, ch_nsin = _dft_tables(FGROUP_DIM, 1.0 / math.sqrt(FGROUP_DIM))
    ch_tab = jnp.concatenate([ch_cos, -ch_nsin], axis=1)
    ctx_cos, ctx_nsin = _dft_tables(SEQ, 1.0 / math.sqrt(SEQ))
    lat_cos, lat_nsin = _latent_dft_tables()

    ckv_states = []
    kpe_states = []
    for l in range(DEPTH):
        j = l // 2
        x = _ffn(xs if l == 0 else (x,), modt, l, 0, norm_g[l, 0], wg, wu, wd, final_norm, False)
        if l % 2 == 0:
            wuq = mla_w_uq[j]
            wuq_p = jnp.concatenate(
                [_pad_heads(wuq, slice(0, QK_NOPE), QK_NOPE),
                 _pad_heads(wuq, slice(QK_NOPE, QK_NOPE + QK_ROPE), ROPE_PAD)], axis=-1)
            wuq_p = wuq_p.reshape(Q_LORA, N_HEADS * HEAD_PAD).astype(BF16)
            wdkv_p = jnp.pad(mla_w_dkv[j], ((0, 0), (0, ROPE_PAD - QK_ROPE))).astype(BF16)
            wukv = mla_w_ukv[j]
            wuk = _pad_heads(wukv, slice(0, QK_NOPE), QK_NOPE).reshape(KV_LORA, -1).astype(BF16)
            wuvt = _pad_heads(wukv, slice(QK_NOPE, QK_NOPE + V_DIM), V_DIM).reshape(KV_LORA, -1).T.astype(BF16)
            q, k, vt, new_ckv, new_kpe = _mla_proj(
                x, modt, l, norm_g[l, 1], mla_w_dq[j].astype(BF16), mla_q_norm[j], wuq_p, wdkv_p,
                mla_kv_norm[j], wuk, wuvt, rope_cos, rope_sin)
            ckv_states.append(new_ckv)
            kpe_states.append(new_kpe)
            kc, vct = _cache_kv(
                cache_ckv[:, j].reshape(DEC_BATCH * PAST_LEN, KV_LORA),
                jnp.pad(cache_kpe[:, j].reshape(DEC_BATCH * PAST_LEN, QK_ROPE),
                        ((0, 0), (0, ROPE_PAD - QK_ROPE))),
                wuk, wuvt)
            o_ctx = _ctx_attn(q, k, vt)
            o_lat = _lat_attn(q, k, vt, kc, vct)
            mixed = (_oproj(x, o_ctx, o_lat, mla_w_o[j].astype(BF16), modt, l),)
        else:
            a, b = _fourier_ch(x, modt, l, norm_g[l, 1], ch_tab)
            fw = fourier_w[j].astype(BF16)
            mixed = (_fourier_ctx(a, b, ctx_cos, ctx_nsin, x, fw, fourier_b[j], modt, l),
                     _fourier_lat(a, b, lat_cos, lat_nsin, x, fw, fourier_b[j], modt, l))
        x = _ffn(mixed, modt, l, 1, norm_g[l, 2], wg, wu, wd, final_norm, l == DEPTH - 1)

    y_ctx, y_lat = x
    return (y_ctx.reshape(BATCH, SEQ, D_MODEL), y_lat.reshape(DEC_BATCH, DEC_SEQ, D_MODEL),
            jnp.stack(ckv_states, axis=1), jnp.stack(kpe_states, axis=1))
```

```python
import functools
import math

import jax
import jax.numpy as jnp
from jax import lax
from jax.experimental import pallas as pl
from jax.experimental.pallas import tpu as pltpu

F32 = jnp.float32
BF16 = jnp.bfloat16

D_MODEL = 1024
BATCH = 32
SEQ = 256
DEPTH = 4
DEC_BATCH = 2
DEC_SEQ = 4096
PAST_LEN = 512
GRID_W = 64
N_HEADS = 8
QK_NOPE = 128
QK_ROPE = 64
V_DIM = 128
Q_LORA = 512
KV_LORA = 256
AXIS_FREQS = QK_ROPE // 4
ROPE_THETA = 10000.0
N_FGROUPS = 4
FGROUP_DIM = D_MODEL // N_FGROUPS
D_FF = 2816
N_MOD = 9
EPS = 1e-6
ATTN_SCALE = 1.0 / math.sqrt(QK_NOPE + QK_ROPE)
Q_SCALE = ATTN_SCALE * math.log2(math.e)

N_CTX = BATCH * SEQ
N_LAT = DEC_BATCH * DEC_SEQ
N_TOK = N_CTX + N_LAT
N_GROUPS = 8
HEAD_PAD = 256
ROPE_PAD = 128
VMEM_LIMIT = 56 * 1024 * 1024


def _cparams(sem):
    return pltpu.CompilerParams(dimension_semantics=sem, vmem_limit_bytes=VMEM_LIMIT)


def _group(i, tm):
    n_ctx = N_CTX // tm
    per_batch = DEC_SEQ // tm
    return jnp.where(i < n_ctx, 0, 1 + (i - n_ctx) // per_batch)


def _rms(x, g):
    return x * lax.rsqrt(jnp.mean(x * x, axis=-1, keepdims=True) + EPS) * g


def _mod_rms(x, g, mod, k):
    gain = g * (1.0 + mod[k + 1:k + 2, :])
    return x * lax.rsqrt(jnp.mean(x * x, axis=-1, keepdims=True) + EPS) * gain + mod[k:k + 1, :]


def _mod_kernel(c_ref, w_ref, b_ref, o_ref):
    c = c_ref[...]
    s = (c * jax.nn.sigmoid(c)).astype(BF16)
    o_ref[...] = jnp.dot(s, w_ref[...].astype(BF16), preferred_element_type=F32) + b_ref[...]


def _modulation(cond, w_mod, b_mod):
    out = pl.pallas_call(
        _mod_kernel,
        out_shape=jax.ShapeDtypeStruct((DEPTH, N_MOD, N_GROUPS, D_MODEL), F32),
        grid=(DEPTH, N_MOD),
        in_specs=[
            pl.BlockSpec((N_GROUPS, D_MODEL), lambda l, k: (0, 0)),
            pl.BlockSpec((None, D_MODEL, D_MODEL), lambda l, k: (l, 0, k)),
            pl.BlockSpec((None, None, 1, D_MODEL), lambda l, k: (l, k, 0, 0)),
        ],
        out_specs=pl.BlockSpec((None, None, N_GROUPS, D_MODEL), lambda l, k: (l, k, 0, 0)),
        compiler_params=_cparams(("arbitrary", "arbitrary")),
        name="adaln_modulation",
    )(cond, w_mod, b_mod.reshape(DEPTH, N_MOD, 1, D_MODEL))
    return out.transpose(0, 2, 1, 3)


FFN_TM = 512
FFN_TF = 256
FFN_SUB = 256
FFN_CTX_TILES = N_CTX // FFN_TM


def _ffn_kernel(*refs, k0, split_in, final):
    n_x = 2 if split_in else 1
    x_refs, (mod_ref, g_ref, wg_ref, wu_ref, wd_ref, fn_ref) = refs[:n_x], refs[n_x:n_x + 6]
    n_out = 2 if final else 1
    out_refs, (a_ref,) = refs[n_x + 6:n_x + 6 + n_out], refs[n_x + 6 + n_out:]
    is_ctx = pl.program_id(0) < FFN_CTX_TILES

    ys = []
    for r in range(FFN_TM // FFN_SUB):
        rows = slice(r * FFN_SUB, (r + 1) * FFN_SUB)
        if split_in:
            x = jnp.where(is_ctx, x_refs[0][rows, :], x_refs[1][rows, :])
        else:
            x = x_refs[0][rows, :]
        h = _mod_rms(x, g_ref[...], mod_ref, k0).astype(BF16)
        for c in range(D_FF // FFN_TF):
            cols = slice(c * FFN_TF, (c + 1) * FFN_TF)
            g = jnp.dot(h, wg_ref[:, cols], preferred_element_type=F32)
            u = jnp.dot(h, wu_ref[:, cols], preferred_element_type=F32)
            a_ref[rows, cols] = (g * jax.nn.sigmoid(g) * u).astype(BF16)
        y = jnp.dot(a_ref[rows, :], wd_ref[...], preferred_element_type=F32)
        y = x + (0.5 * mod_ref[k0 + 2:k0 + 3, :]) * y
        if final:
            ys.append((rows, _rms(y, fn_ref[...])))
        else:
            out_refs[0][rows, :] = y

    if final:
        @pl.when(is_ctx)
        def _():
            for rows, y in ys:
                out_refs[0][rows, :] = y

        @pl.when(jnp.logical_not(is_ctx))
        def _():
            for rows, y in ys:
                out_refs[1][rows, :] = y


def _ffn(xs, modt, l, half, g, wg, wu, wd, final_norm, final):
    tm = FFN_TM
    k0 = 6 * half
    split_in = len(xs) == 2
    rows = pl.BlockSpec((tm, D_MODEL), lambda i: (i, 0))
    ctx_rows = pl.BlockSpec((tm, D_MODEL), lambda i: (jnp.minimum(i, FFN_CTX_TILES - 1), 0))
    lat_rows = pl.BlockSpec((tm, D_MODEL), lambda i: (jnp.maximum(i - FFN_CTX_TILES, 0), 0))
    resident = lambda shape: pl.BlockSpec((None, None) + shape, lambda i: (l, half, 0, 0),
                                          pipeline_mode=pl.Buffered(1))
    if final:
        out_shape = (jax.ShapeDtypeStruct((N_CTX, D_MODEL), F32),
                     jax.ShapeDtypeStruct((N_LAT, D_MODEL), F32))
        out_specs = (ctx_rows, lat_rows)
    else:
        out_shape = jax.ShapeDtypeStruct((N_TOK, D_MODEL), F32)
        out_specs = rows
    return pl.pallas_call(
        functools.partial(_ffn_kernel, k0=k0, split_in=split_in, final=final),
        out_shape=out_shape,
        grid=(N_TOK // tm,),
        in_specs=([ctx_rows, lat_rows] if split_in else [rows]) + [
            pl.BlockSpec((None, None, N_MOD, D_MODEL), lambda i: (l, _group(i, tm), 0, 0)),
            pl.BlockSpec((1, D_MODEL), lambda i: (0, 0)),
            resident((D_MODEL, D_FF)),
            resident((D_MODEL, D_FF)),
            resident((D_FF, D_MODEL)),
            pl.BlockSpec((1, D_MODEL), lambda i: (0, 0)),
        ],
        out_specs=out_specs,
        scratch_shapes=[pltpu.VMEM((tm, D_FF), BF16)],
        compiler_params=_cparams(("arbitrary",)),
        name="swiglu_halfstep",
    )(*xs, modt, g.reshape(1, D_MODEL), wg, wu, wd, final_norm.reshape(1, D_MODEL))


MLA_TM = 512


def _swap_halves(x):
    lane = lax.broadcasted_iota(jnp.int32, x.shape, 1)
    fwd = pltpu.roll(x, ROPE_PAD - AXIS_FREQS, axis=1)
    bwd = pltpu.roll(x, AXIS_FREQS, axis=1)
    return jnp.where(lane % (2 * AXIS_FREQS) < AXIS_FREQS, fwd, bwd)


def _rope(x, cos, sin):
    return x * cos + _swap_halves(x) * sin


MLA_SUB = SEQ


def _kv_up(ckv, kpe_pad, wuk_ref, wuvt_ref, k_ref, vt_ref, rows):
    c = ckv.astype(BF16)
    kn = jnp.dot(c, wuk_ref[...], preferred_element_type=F32).astype(BF16)
    vt = lax.dot_general(wuvt_ref[...], c, _NT, preferred_element_type=F32)
    vt_ref[:, rows] = vt.astype(BF16)
    kp = kpe_pad.astype(BF16)
    for hd in range(N_HEADS):
        k_ref[rows, hd * HEAD_PAD:hd * HEAD_PAD + QK_NOPE] = kn[:, hd * QK_NOPE:(hd + 1) * QK_NOPE]
        k_ref[rows, hd * HEAD_PAD + QK_NOPE:(hd + 1) * HEAD_PAD] = kp
    return vt


def _mla_proj_kernel(x_ref, mod_ref, g_ref, wdq_ref, qn_ref, wuq_ref, wdkv_ref, kvn_ref, wuk_ref,
                     wuvt_ref, cos_ref, sin_ref, q_ref, k_ref, vt_ref, v_ref, ckv_ref, kpe_ref):
    states = []
    for r in range(MLA_TM // MLA_SUB):
        rows = slice(r * MLA_SUB, (r + 1) * MLA_SUB)
        h = _mod_rms(x_ref[rows, :], g_ref[...], mod_ref, 3).astype(BF16)
        cq = _rms(jnp.dot(h, wdq_ref[...], preferred_element_type=F32), qn_ref[...]).astype(BF16)
        q = jnp.dot(cq, wuq_ref[...], preferred_element_type=F32)
        cos = cos_ref[rows, :]
        sin = sin_ref[rows, :]
        for hd in range(N_HEADS):
            lo = hd * HEAD_PAD
            q_ref[rows, lo:lo + QK_NOPE] = (q[:, lo:lo + QK_NOPE] * Q_SCALE).astype(BF16)
            qr = _rope(q[:, lo + QK_NOPE:lo + HEAD_PAD], cos, sin)
            q_ref[rows, lo + QK_NOPE:lo + HEAD_PAD] = (qr * Q_SCALE).astype(BF16)
        kv = jnp.dot(h, wdkv_ref[...], preferred_element_type=F32)
        ckv = _rms(kv[:, :KV_LORA], kvn_ref[...])
        kpe = _rope(kv[:, KV_LORA:], cos, sin)
        vt = _kv_up(ckv, kpe, wuk_ref, wuvt_ref, k_ref, vt_ref, rows)
        states.append((rows, ckv, kpe[:, :QK_ROPE], vt))

    @pl.when(pl.program_id(0) < N_CTX // MLA_TM)
    def _():
        for r, (rows, ckv, kpe, vt) in enumerate(states):
            ckv_ref[r] = ckv
            kpe_ref[r] = kpe
            v_ref[rows, :] = vt.T.astype(BF16)


def _mla_proj(x, modt, l, g, wdq, qn, wuq, wdkv, kvn, wuk, wuvt, cos_t, sin_t):
    tm = MLA_TM
    n_ctx = N_CTX // tm
    per_batch = DEC_SEQ // tm

    def rope_idx(i):
        return jnp.where(i < n_ctx, per_batch, (i - n_ctx) % per_batch)

    full = lambda shape: pl.BlockSpec(shape, lambda i: (0, 0))
    rows = lambda w: pl.BlockSpec((tm, w), lambda i: (i, 0))
    state = lambda w: pl.BlockSpec((tm // SEQ, SEQ, w),
                                   lambda i: (jnp.minimum(i, n_ctx - 1), 0, 0))
    return pl.pallas_call(
        _mla_proj_kernel,
        out_shape=(
            jax.ShapeDtypeStruct((N_TOK, N_HEADS * HEAD_PAD), BF16),
            jax.ShapeDtypeStruct((N_TOK, N_HEADS * HEAD_PAD), BF16),
            jax.ShapeDtypeStruct((N_HEADS * V_DIM, N_TOK), BF16),
            jax.ShapeDtypeStruct((N_CTX, N_HEADS * V_DIM), BF16),
            jax.ShapeDtypeStruct((BATCH, SEQ, KV_LORA), F32),
            jax.ShapeDtypeStruct((BATCH, SEQ, QK_ROPE), F32),
        ),
        grid=(N_TOK // tm,),
        in_specs=[
            rows(D_MODEL),
            pl.BlockSpec((None, None, N_MOD, D_MODEL), lambda i: (l, _group(i, tm), 0, 0)),
            full((1, D_MODEL)),
            full((D_MODEL, Q_LORA)),
            full((1, Q_LORA)),
            full((Q_LORA, N_HEADS * HEAD_PAD)),
            full((D_MODEL, KV_LORA + ROPE_PAD)),
            full((1, KV_LORA)),
            full((KV_LORA, N_HEADS * QK_NOPE)),
            full((N_HEADS * V_DIM, KV_LORA)),
            pl.BlockSpec((tm, ROPE_PAD), lambda i: (rope_idx(i), 0)),
            pl.BlockSpec((tm, ROPE_PAD), lambda i: (rope_idx(i), 0)),
        ],
        out_specs=(rows(N_HEADS * HEAD_PAD), rows(N_HEADS * HEAD_PAD),
                   pl.BlockSpec((N_HEADS * V_DIM, tm), lambda i: (0, i)),
                   pl.BlockSpec((tm, N_HEADS * V_DIM), lambda i: (jnp.minimum(i, n_ctx - 1), 0)),
                   state(KV_LORA), state(QK_ROPE)),
        compiler_params=_cparams(("arbitrary",)),
        name="mla_projections",
    )(x, modt, g.reshape(1, D_MODEL), wdq, qn.reshape(1, Q_LORA), wuq, wdkv,
      kvn.reshape(1, KV_LORA), wuk, wuvt, cos_t, sin_t)


def _cache_kv_kernel(ckv_ref, kpe_ref, wuk_ref, wuvt_ref, k_ref, vt_ref):
    _kv_up(ckv_ref[...], kpe_ref[...], wuk_ref, wuvt_ref, k_ref, vt_ref, slice(None))


def _cache_kv(ckv, kpe_pad, wuk, wuvt):
    n = DEC_BATCH * PAST_LEN
    tm = PAST_LEN
    full = lambda shape: pl.BlockSpec(shape, lambda i: (0, 0))
    rows = lambda w: pl.BlockSpec((tm, w), lambda i: (i, 0))
    return pl.pallas_call(
        _cache_kv_kernel,
        out_shape=(jax.ShapeDtypeStruct((n, N_HEADS * HEAD_PAD), BF16),
                   jax.ShapeDtypeStruct((N_HEADS * V_DIM, n), BF16)),
        grid=(n // tm,),
        in_specs=[rows(KV_LORA), rows(ROPE_PAD),
                  full((KV_LORA, N_HEADS * QK_NOPE)), full((N_HEADS * V_DIM, KV_LORA))],
        out_specs=(rows(N_HEADS * HEAD_PAD), pl.BlockSpec((N_HEADS * V_DIM, tm), lambda i: (0, i))),
        compiler_params=_cparams(("parallel",)),
        name="cache_kv_up",
    )(ckv, kpe_pad, wuk, wuvt)


_NT = (((1,), (1,)), ((), ()))


def _ctx_attn_kernel(q_ref, k_ref, v_ref, o_ref):
    for hd in range(N_HEADS):
        q = q_ref[:, hd * HEAD_PAD:(hd + 1) * HEAD_PAD]
        k = k_ref[:, hd * HEAD_PAD:(hd + 1) * HEAD_PAD]
        s = lax.dot_general(q, k, _NT, preferred_element_type=F32)
        p = jnp.exp2(s - jnp.max(s, axis=-1, keepdims=True))
        o = jnp.dot(p.astype(BF16), v_ref[:, hd * V_DIM:(hd + 1) * V_DIM],
                    preferred_element_type=F32)
        o = o / jnp.sum(p, axis=-1, keepdims=True)
        o_ref[:, hd * V_DIM:(hd + 1) * V_DIM] = o.astype(BF16)


def _ctx_attn(q, k, v):
    rows = lambda w: pl.BlockSpec((SEQ, w), lambda b: (b, 0))
    return pl.pallas_call(
        _ctx_attn_kernel,
        out_shape=jax.ShapeDtypeStruct((N_CTX, N_HEADS * V_DIM), BF16),
        grid=(BATCH,),
        in_specs=[rows(N_HEADS * HEAD_PAD), rows(N_HEADS * HEAD_PAD), rows(N_HEADS * V_DIM)],
        out_specs=rows(N_HEADS * V_DIM),
        compiler_params=_cparams(("parallel",)),
        name="context_attention",
    )(q, k, v)


LAT_TQ = 256
LAT_KC = 512
LAT_KEYS = PAST_LEN + DEC_SEQ
SUBLANES = 8


def _lat_attn_kernel(q_ref, kc_ref, kl_ref, vct_ref, vlt_ref, o_ref, s_ref, m_ref):
    tq = q_ref.shape[0]
    kc = LAT_KC
    n_cache = PAST_LEN // kc
    n_chunks = LAT_KEYS // kc
    step = pl.program_id(2)

    @pl.when((pl.program_id(0) == 0) & (pl.program_id(1) == 0) & (step == 0))
    def _():
        s_ref[1] = jnp.zeros(s_ref.shape[1:], F32)
        m_ref[1] = jnp.zeros(m_ref.shape[1:], F32)

    def rows8(x, op):
        return op(x.reshape(kc // SUBLANES, SUBLANES, tq), axis=0)

    def body(cur):
        prev = 1 - cur
        q = q_ref[...]
        m = jnp.max(m_ref[prev], axis=0, keepdims=True)
        mpart = None
        lpart = jnp.zeros((SUBLANES, tq), F32)
        acc = jnp.zeros((V_DIM, tq), F32)
        for c in range(n_chunks):
            rows = slice(c * kc, (c + 1) * kc)
            if c < n_cache:
                k = kc_ref[rows, :]
                vt = vct_ref[:, rows]
            else:
                lat = slice((c - n_cache) * kc, (c - n_cache + 1) * kc)
                k = kl_ref[lat, :]
                vt = vlt_ref[:, lat]
            st = lax.dot_general(k, q, _NT, preferred_element_type=F32)
            s_ref[cur, rows, :] = st
            cm = rows8(st, jnp.max)
            mpart = cm if mpart is None else jnp.maximum(mpart, cm)
            p = jnp.exp2(s_ref[prev, rows, :] - m)
            lpart = lpart + rows8(p, jnp.sum)
            acc = acc + jnp.dot(vt, p.astype(BF16), preferred_element_type=F32)
        m_ref[cur] = mpart
        denom = jnp.sum(lpart, axis=0, keepdims=True)
        o_ref[...] = (acc / denom).T.astype(BF16)

    for parity in range(2):
        pl.when(step % 2 == parity)(functools.partial(body, parity))


def _lat_attn(q, k, vt, kc, vct):
    tq = LAT_TQ
    q_off = N_CTX // tq
    l_off = N_CTX // DEC_SEQ
    nq = DEC_SEQ // tq
    return pl.pallas_call(
        _lat_attn_kernel,
        out_shape=jax.ShapeDtypeStruct((N_LAT, N_HEADS * V_DIM), BF16),
        grid=(DEC_BATCH, N_HEADS, nq + 1),
        in_specs=[
            pl.BlockSpec((tq, HEAD_PAD),
                         lambda b, h, i: (q_off + b * nq + jnp.minimum(i, nq - 1), h)),
            pl.BlockSpec((PAST_LEN, HEAD_PAD), lambda b, h, i: (b, h)),
            pl.BlockSpec((DEC_SEQ, HEAD_PAD), lambda b, h, i: (l_off + b, h)),
            pl.BlockSpec((V_DIM, PAST_LEN), lambda b, h, i: (h, b)),
            pl.BlockSpec((V_DIM, DEC_SEQ), lambda b, h, i: (h, l_off + b)),
        ],
        out_specs=pl.BlockSpec((tq, V_DIM),
                               lambda b, h, i: (b * nq + jnp.maximum(i - 1, 0), h)),
        scratch_shapes=[pltpu.VMEM((2, LAT_KEYS, tq), F32), pltpu.VMEM((2, SUBLANES, tq), F32)],
        compiler_params=_cparams(("arbitrary", "arbitrary", "arbitrary")),
        name="latent_attention",
    )(q, kc, k, vct, vt)


OPROJ_TM = 1024


def _oproj_kernel(x_ref, oc_ref, ol_ref, w_ref, mod_ref, y_ref):
    o = jnp.where(pl.program_id(0) < N_CTX // OPROJ_TM, oc_ref[...], ol_ref[...])
    y_ref[...] = x_ref[...] + mod_ref[5:6, :] * jnp.dot(o, w_ref[...], preferred_element_type=F32)


def _oproj(x, o_ctx, o_lat, w, modt, l):
    tm = OPROJ_TM
    n_ctx = N_CTX // tm
    return pl.pallas_call(
        _oproj_kernel,
        out_shape=jax.ShapeDtypeStruct((N_TOK, D_MODEL), F32),
        grid=(N_TOK // tm,),
        in_specs=[
            pl.BlockSpec((tm, D_MODEL), lambda i: (i, 0)),
            pl.BlockSpec((tm, N_HEADS * V_DIM), lambda i: (jnp.minimum(i, n_ctx - 1), 0)),
            pl.BlockSpec((tm, N_HEADS * V_DIM), lambda i: (jnp.maximum(i - n_ctx, 0), 0)),
            pl.BlockSpec((N_HEADS * V_DIM, D_MODEL), lambda i: (0, 0)),
            pl.BlockSpec((None, None, N_MOD, D_MODEL), lambda i: (l, _group(i, tm), 0, 0)),
        ],
        out_specs=pl.BlockSpec((tm, D_MODEL), lambda i: (i, 0)),
        compiler_params=_cparams(("parallel",)),
        name="attn_out_proj",
    )(x, o_ctx, o_lat, w, modt)


FCH_TM = 1024


def _fourier_ch_kernel(x_ref, mod_ref, g_ref, cs_ref, a_ref, b_ref):
    h = _mod_rms(x_ref[...], g_ref[...], mod_ref, 3).astype(BF16)
    for gi in range(N_FGROUPS):
        lo = gi * FGROUP_DIM
        r = jnp.dot(h[:, lo:lo + FGROUP_DIM], cs_ref[...], preferred_element_type=F32)
        a_ref[:, lo:lo + FGROUP_DIM] = r[:, :FGROUP_DIM].astype(BF16)
        b_ref[:, lo:lo + FGROUP_DIM] = r[:, FGROUP_DIM:].astype(BF16)


def _fourier_ch(x, modt, l, g, ch_tab):
    tm = FCH_TM
    rows = pl.BlockSpec((tm, D_MODEL), lambda i: (i, 0))
    return pl.pallas_call(
        _fourier_ch_kernel,
        out_shape=(jax.ShapeDtypeStruct((N_TOK, D_MODEL), BF16),
                   jax.ShapeDtypeStruct((N_TOK, D_MODEL), BF16)),
        grid=(N_TOK // tm,),
        in_specs=[
            rows,
            pl.BlockSpec((None, None, N_MOD, D_MODEL), lambda i: (l, _group(i, tm), 0, 0)),
            pl.BlockSpec((1, D_MODEL), lambda i: (0, 0)),
            pl.BlockSpec((FGROUP_DIM, 2 * FGROUP_DIM), lambda i: (0, 0)),
        ],
        out_specs=(rows, rows),
        compiler_params=_cparams(("parallel",)),
        name="fourier_channel_dft",
    )(x, modt, g.reshape(1, D_MODEL), ch_tab)


def _fourier_epilogue(f, x_ref, w_ref, bias_ref, mod_ref):
    mixed = jnp.dot(f.astype(BF16), w_ref[...], preferred_element_type=F32) + bias_ref[...]
    return x_ref[...] + mod_ref[5:6, :] * mixed


def _fourier_ctx_kernel(a_ref, b_ref, c_ref, s_ref, x_ref, w_ref, bias_ref, mod_ref, y_ref):
    f = jnp.dot(c_ref[...], a_ref[...], preferred_element_type=F32)
    f = f + jnp.dot(s_ref[...], b_ref[...], preferred_element_type=F32)
    y_ref[...] = _fourier_epilogue(f, x_ref, w_ref, bias_ref, mod_ref)


def _fourier_ctx(a, b, cos_t, nsin_t, x, w, bias, modt, l):
    rows = lambda dt: pl.BlockSpec((SEQ, D_MODEL), lambda i: (i, 0))
    full = lambda shape: pl.BlockSpec(shape, lambda i: (0, 0))
    return pl.pallas_call(
        _fourier_ctx_kernel,
        out_shape=jax.ShapeDtypeStruct((N_CTX, D_MODEL), F32),
        grid=(BATCH,),
        in_specs=[rows(BF16), rows(BF16), full((SEQ, SEQ)), full((SEQ, SEQ)), rows(F32),
                  full((D_MODEL, D_MODEL)), full((1, D_MODEL)),
                  pl.BlockSpec((None, None, N_MOD, D_MODEL), lambda i: (l, 0, 0, 0))],
        out_specs=rows(F32),
        compiler_params=_cparams(("parallel",)),
        name="fourier_context_positions",
    )(a, b, cos_t, nsin_t, x, w, bias.reshape(1, D_MODEL), modt)


FLAT_TM = 512


def _fourier_lat(a, b, cos_t, nsin_t, x, w, bias, modt, l):
    tm = FLAT_TM
    ni = DEC_SEQ // tm
    r_off = N_CTX // tm
    b_off = N_CTX // DEC_SEQ
    src = pl.BlockSpec((DEC_SEQ, D_MODEL), lambda bb, i: (b_off + bb, 0),
                       pipeline_mode=pl.Buffered(1))
    tab = pl.BlockSpec((tm, DEC_SEQ), lambda bb, i: (i, 0))
    return pl.pallas_call(
        _fourier_ctx_kernel,
        out_shape=jax.ShapeDtypeStruct((N_LAT, D_MODEL), F32),
        grid=(DEC_BATCH, ni),
        in_specs=[src, src, tab, tab,
                  pl.BlockSpec((tm, D_MODEL), lambda bb, i: (r_off + bb * ni + i, 0)),
                  pl.BlockSpec((D_MODEL, D_MODEL), lambda bb, i: (0, 0)),
                  pl.BlockSpec((1, D_MODEL), lambda bb, i: (0, 0)),
                  pl.BlockSpec((None, None, N_MOD, D_MODEL), lambda bb, i: (l, 1 + bb, 0, 0))],
        out_specs=pl.BlockSpec((tm, D_MODEL), lambda bb, i: (bb * ni + i, 0)),
        compiler_params=_cparams(("arbitrary", "arbitrary")),
        name="fourier_latent_positions",
    )(a, b, cos_t, nsin_t, x, w, bias.reshape(1, D_MODEL), modt)


TAB_R = 64
TAB_TM = 256


def _dft_table_kernel(ar_ref, ai_ref, br_ref, bi_ref, cos_ref, nsin_ref):
    br = br_ref[...]
    bi = bi_ref[...]
    for r in range(TAB_TM // TAB_R):
        ar = ar_ref[r]
        ai = ai_ref[r]
        rows = slice(r * TAB_R, (r + 1) * TAB_R)
        cos_ref[rows, :] = (ar * br - ai * bi).astype(BF16)
        nsin_ref[rows, :] = (ai * br + ar * bi).astype(BF16)


def _latent_dft_tables():
    n = DEC_SEQ
    assert n == TAB_R * TAB_R
    idx = jnp.arange(TAB_R, dtype=jnp.int32)
    prod = idx[:, None] * idx[None, :]
    ang_c = (prod % TAB_R).astype(F32) * (2.0 * math.pi / TAB_R)
    ang_f = prod.astype(F32) * (2.0 * math.pi / n)
    cr, ci = jnp.cos(ang_c), -jnp.sin(ang_c)
    scale = 1.0 / math.sqrt(n)
    fr, fi = jnp.cos(ang_f) * scale, jnp.sin(ang_f) * (-scale)
    ar = jnp.tile(cr, (1, TAB_R)).reshape(TAB_R, 1, n)
    ai = jnp.tile(ci, (1, TAB_R)).reshape(TAB_R, 1, n)
    br = (cr[:, :, None] * fr[:, None, :] - ci[:, :, None] * fi[:, None, :]).reshape(TAB_R, n)
    bi = (cr[:, :, None] * fi[:, None, :] + ci[:, :, None] * fr[:, None, :]).reshape(TAB_R, n)
    per = TAB_TM // TAB_R
    coarse = pl.BlockSpec((per, 1, n), lambda i: (i, 0, 0))
    fine = pl.BlockSpec((TAB_R, n), lambda i: (0, 0))
    out = pl.BlockSpec((TAB_TM, n), lambda i: (i, 0))
    return pl.pallas_call(
        _dft_table_kernel,
        out_shape=(jax.ShapeDtypeStruct((n, n), BF16), jax.ShapeDtypeStruct((n, n), BF16)),
        grid=(n // TAB_TM,),
        in_specs=[coarse, coarse, fine, fine],
        out_specs=(out, out),
        compiler_params=_cparams(("parallel",)),
        name="latent_dft_tables",
    )(ar, ai, br, bi)


def _dft_tables(n, scale):
    idx = jnp.arange(n, dtype=jnp.int32)
    ang = ((idx[:, None] * idx[None, :]) % n).astype(F32) * (2.0 * math.pi / n)
    return (jnp.cos(ang) * scale).astype(BF16), (jnp.sin(ang) * (-scale)).astype(BF16)


def _rope_tables(tm):
    rows = DEC_SEQ // GRID_W
    inv = 1.0 / (ROPE_THETA ** (jnp.arange(AXIS_FREQS, dtype=F32) / AXIS_FREQS))
    ar = jnp.arange(rows, dtype=F32)[:, None] * inv
    ac = jnp.arange(GRID_W, dtype=F32)[:, None] * inv
    cos_r, sin_r = (jnp.repeat(t, GRID_W, axis=0) for t in (jnp.cos(ar), jnp.sin(ar)))
    cos_c, sin_c = (jnp.tile(t, (rows, 1)) for t in (jnp.cos(ac), jnp.sin(ac)))
    pad = jnp.zeros((DEC_SEQ, ROPE_PAD - QK_ROPE), F32)
    cos = jnp.concatenate([cos_r, cos_r, cos_c, cos_c, pad + 1.0], axis=1)
    sin = jnp.concatenate([-sin_r, sin_r, -sin_c, sin_c, pad], axis=1)
    cos = jnp.concatenate([cos, jnp.ones((tm, ROPE_PAD), F32)], axis=0)
    sin = jnp.concatenate([sin, jnp.zeros((tm, ROPE_PAD), F32)], axis=0)
    return cos, sin


def _pad_heads(w, part, width):
    k = w.shape[0]
    w = w.reshape(k, N_HEADS, -1)[:, :, part]
    return jnp.pad(w, ((0, 0), (0, 0), (0, width - w.shape[-1])))


def kernel(x_prompt, x_sample, cache_ckv, cache_kpe, c, c_ctx, w_mod, b_mod, norm_g, ffn_wg, ffn_wu,
           ffn_wd, mla_w_dq, mla_q_norm, mla_w_uq, mla_w_dkv, mla_kv_norm, mla_w_ukv, mla_w_o,
           fourier_w, fourier_b, final_norm):
    xs = (x_prompt.reshape(N_CTX, D_MODEL), x_sample.reshape(N_LAT, D_MODEL))
    cond = jnp.concatenate([c_ctx[None, :], c, jnp.zeros((N_GROUPS - 1 - DEC_BATCH, D_MODEL), F32)], axis=0)
    modt = _modulation(cond, w_mod, b_mod)

    wg = ffn_wg.astype(BF16)
    wu = ffn_wu.astype(BF16)
    wd = ffn_wd.astype(BF16)

    rope_cos, rope_sin = _rope_tables(MLA_TM)
    ch_cos, ch_nsin = _dft_tables(FGROUP_DIM, 1.0 / math.sqrt(FGROUP_DIM))
    ch_tab = jnp.concatenate([ch_cos, -ch_nsin], axis=1)
    ctx_cos, ctx_nsin = _dft_tables(SEQ, 1.0 / math.sqrt(SEQ))
    lat_cos, lat_nsin = _latent_dft_tables()

    ckv_states = []
    kpe_states = []
    for l in range(DEPTH):
        j = l // 2
        x = _ffn(xs if l == 0 else (x,), modt, l, 0, norm_g[l, 0], wg, wu, wd, final_norm, False)
        if l % 2 == 0:
            wuq = mla_w_uq[j]
            wuq_p = jnp.concatenate(
                [_pad_heads(wuq, slice(0, QK_NOPE), QK_NOPE),
                 _pad_heads(wuq, slice(QK_NOPE, QK_NOPE + QK_ROPE), ROPE_PAD)], axis=-1)
            wuq_p = wuq_p.reshape(Q_LORA, N_HEADS * HEAD_PAD).astype(BF16)
            wdkv_p = jnp.pad(mla_w_dkv[j], ((0, 0), (0, ROPE_PAD - QK_ROPE))).astype(BF16)
            wukv = mla_w_ukv[j]
            wuk = _pad_heads(wukv, slice(0, QK_NOPE), QK_NOPE).reshape(KV_LORA, -1).astype(BF16)
            wuvt = _pad_heads(wukv, slice(QK_NOPE, QK_NOPE + V_DIM), V_DIM).reshape(KV_LORA, -1).T.astype(BF16)
            q, k, vt, v_ctx, new_ckv, new_kpe = _mla_proj(
                x, modt, l, norm_g[l, 1], mla_w_dq[j].astype(BF16), mla_q_norm[j], wuq_p, wdkv_p,
                mla_kv_norm[j], wuk, wuvt, rope_cos, rope_sin)
            ckv_states.append(new_ckv)
            kpe_states.append(new_kpe)
            kc, vct = _cache_kv(
                cache_ckv[:, j].reshape(DEC_BATCH * PAST_LEN, KV_LORA),
                jnp.pad(cache_kpe[:, j].reshape(DEC_BATCH * PAST_LEN, QK_ROPE),
                        ((0, 0), (0, ROPE_PAD - QK_ROPE))),
                wuk, wuvt)
            o_ctx = _ctx_attn(q, k, v_ctx)
            o_lat = _lat_attn(q, k, vt, kc, vct)
            mixed = (_oproj(x, o_ctx, o_lat, mla_w_o[j].astype(BF16), modt, l),)
        else:
            a, b = _fourier_ch(x, modt, l, norm_g[l, 1], ch_tab)
            fw = fourier_w[j].astype(BF16)
            mixed = (_fourier_ctx(a, b, ctx_cos, ctx_nsin, x, fw, fourier_b[j], modt, l),
                     _fourier_lat(a, b, lat_cos, lat_nsin, x, fw, fourier_b[j], modt, l))
        x = _ffn(mixed, modt, l, 1, norm_g[l, 2], wg, wu, wd, final_norm, l == DEPTH - 1)

    y_ctx, y_lat = x
    return (y_ctx.reshape(BATCH, SEQ, D_MODEL), y_lat.reshape(DEC_BATCH, DEC_SEQ, D_MODEL),
            jnp.stack(ckv_states, axis=1), jnp.stack(kpe_states, axis=1))
```

```python
import functools
import math

import jax
import jax.numpy as jnp
from jax import lax
from jax.experimental import pallas as pl
from jax.experimental.pallas import tpu as pltpu

F32 = jnp.float32
BF16 = jnp.bfloat16

D_MODEL = 1024
BATCH = 32
SEQ = 256
DEPTH = 4
DEC_BATCH = 2
DEC_SEQ = 4096
PAST_LEN = 512
GRID_W = 64
N_HEADS = 8
QK_NOPE = 128
QK_ROPE = 64
V_DIM = 128
Q_LORA = 512
KV_LORA = 256
AXIS_FREQS = QK_ROPE // 4
ROPE_THETA = 10000.0
N_FGROUPS = 4
FGROUP_DIM = D_MODEL // N_FGROUPS
D_FF = 2816
N_MOD = 9
EPS = 1e-6
ATTN_SCALE = 1.0 / math.sqrt(QK_NOPE + QK_ROPE)
Q_SCALE = ATTN_SCALE * math.log2(math.e)

N_CTX = BATCH * SEQ
N_LAT = DEC_BATCH * DEC_SEQ
N_TOK = N_CTX + N_LAT
N_GROUPS = 8
HEAD_PAD = 256
ROPE_PAD = 128
VMEM_LIMIT = 56 * 1024 * 1024


def _cparams(sem):
    return pltpu.CompilerParams(dimension_semantics=sem, vmem_limit_bytes=VMEM_LIMIT)


def _group(i, tm):
    n_ctx = N_CTX // tm
    per_batch = DEC_SEQ // tm
    return jnp.where(i < n_ctx, 0, 1 + (i - n_ctx) // per_batch)


def _rms(x, g):
    return x * lax.rsqrt(jnp.mean(x * x, axis=-1, keepdims=True) + EPS) * g


def _mod_rms(x, g, mod, k):
    gain = g * (1.0 + mod[k + 1:k + 2, :])
    return x * lax.rsqrt(jnp.mean(x * x, axis=-1, keepdims=True) + EPS) * gain + mod[k:k + 1, :]


def _mod_kernel(c_ref, w_ref, b_ref, o_ref):
    c = c_ref[...]
    s = (c * jax.nn.sigmoid(c)).astype(BF16)
    o_ref[...] = jnp.dot(s, w_ref[...].astype(BF16), preferred_element_type=F32) + b_ref[...]


def _modulation(cond, w_mod, b_mod):
    out = pl.pallas_call(
        _mod_kernel,
        out_shape=jax.ShapeDtypeStruct((DEPTH, N_MOD, N_GROUPS, D_MODEL), F32),
        grid=(DEPTH, N_MOD),
        in_specs=[
            pl.BlockSpec((N_GROUPS, D_MODEL), lambda l, k: (0, 0)),
            pl.BlockSpec((None, D_MODEL, D_MODEL), lambda l, k: (l, 0, k)),
            pl.BlockSpec((None, None, 1, D_MODEL), lambda l, k: (l, k, 0, 0)),
        ],
        out_specs=pl.BlockSpec((None, None, N_GROUPS, D_MODEL), lambda l, k: (l, k, 0, 0)),
        compiler_params=_cparams(("arbitrary", "arbitrary")),
        name="adaln_modulation",
    )(cond, w_mod, b_mod.reshape(DEPTH, N_MOD, 1, D_MODEL))
    return out.transpose(0, 2, 1, 3)


FFN_TM = 512
FFN_TF = 256
FFN_SUB = 256
FFN_CTX_TILES = N_CTX // FFN_TM


def _ffn_kernel(*refs, k0, split_in, attn, final):
    n_x = 2 if split_in else 1
    n_in = n_x + (3 if attn else 0)
    x_refs, attn_refs = refs[:n_x], refs[n_x:n_in]
    mod_ref, g_ref, wg_ref, wu_ref, wd_ref, fn_ref = refs[n_in:n_in + 6]
    n_out = 2 if final else 1
    out_refs, (a_ref,) = refs[n_in + 6:n_in + 6 + n_out], refs[n_in + 6 + n_out:]
    is_ctx = pl.program_id(0) < FFN_CTX_TILES

    ys = []
    for r in range(FFN_TM // FFN_SUB):
        rows = slice(r * FFN_SUB, (r + 1) * FFN_SUB)
        if split_in:
            x = jnp.where(is_ctx, x_refs[0][rows, :], x_refs[1][rows, :])
        else:
            x = x_refs[0][rows, :]
        if attn:
            oc_ref, ol_ref, wo_ref = attn_refs
            o = jnp.where(is_ctx, oc_ref[rows, :], ol_ref[rows, :])
            x = x + mod_ref[5:6, :] * jnp.dot(o, wo_ref[...], preferred_element_type=F32)
        h = _mod_rms(x, g_ref[...], mod_ref, k0).astype(BF16)
        for c in range(D_FF // FFN_TF):
            cols = slice(c * FFN_TF, (c + 1) * FFN_TF)
            g = jnp.dot(h, wg_ref[:, cols], preferred_element_type=F32)
            u = jnp.dot(h, wu_ref[:, cols], preferred_element_type=F32)
            a_ref[rows, cols] = (g * jax.nn.sigmoid(g) * u).astype(BF16)
        y = jnp.dot(a_ref[rows, :], wd_ref[...], preferred_element_type=F32)
        y = x + (0.5 * mod_ref[k0 + 2:k0 + 3, :]) * y
        if final:
            ys.append((rows, _rms(y, fn_ref[...])))
        else:
            out_refs[0][rows, :] = y

    if final:
        @pl.when(is_ctx)
        def _():
            for rows, y in ys:
                out_refs[0][rows, :] = y

        @pl.when(jnp.logical_not(is_ctx))
        def _():
            for rows, y in ys:
                out_refs[1][rows, :] = y


def _ffn(xs, modt, l, half, g, wg, wu, wd, final_norm, final, attn=()):
    tm = FFN_TM
    k0 = 6 * half
    split_in = len(xs) == 2
    o_width = N_HEADS * V_DIM
    attn_specs = [
        pl.BlockSpec((tm, o_width), lambda i: (jnp.minimum(i, FFN_CTX_TILES - 1), 0)),
        pl.BlockSpec((tm, o_width), lambda i: (jnp.maximum(i - FFN_CTX_TILES, 0), 0)),
        pl.BlockSpec((o_width, D_MODEL), lambda i: (0, 0), pipeline_mode=pl.Buffered(1)),
    ] if attn else []
    rows = pl.BlockSpec((tm, D_MODEL), lambda i: (i, 0))
    ctx_rows = pl.BlockSpec((tm, D_MODEL), lambda i: (jnp.minimum(i, FFN_CTX_TILES - 1), 0))
    lat_rows = pl.BlockSpec((tm, D_MODEL), lambda i: (jnp.maximum(i - FFN_CTX_TILES, 0), 0))
    resident = lambda shape: pl.BlockSpec((None, None) + shape, lambda i: (l, half, 0, 0),
                                          pipeline_mode=pl.Buffered(1))
    if final:
        out_shape = (jax.ShapeDtypeStruct((N_CTX, D_MODEL), F32),
                     jax.ShapeDtypeStruct((N_LAT, D_MODEL), F32))
        out_specs = (ctx_rows, lat_rows)
    else:
        out_shape = jax.ShapeDtypeStruct((N_TOK, D_MODEL), F32)
        out_specs = rows
    return pl.pallas_call(
        functools.partial(_ffn_kernel, k0=k0, split_in=split_in, attn=bool(attn), final=final),
        out_shape=out_shape,
        grid=(N_TOK // tm,),
        in_specs=([ctx_rows, lat_rows] if split_in else [rows]) + attn_specs + [
            pl.BlockSpec((None, None, N_MOD, D_MODEL), lambda i: (l, _group(i, tm), 0, 0)),
            pl.BlockSpec((1, D_MODEL), lambda i: (0, 0)),
            resident((D_MODEL, D_FF)),
            resident((D_MODEL, D_FF)),
            resident((D_FF, D_MODEL)),
            pl.BlockSpec((1, D_MODEL), lambda i: (0, 0)),
        ],
        out_specs=out_specs,
        scratch_shapes=[pltpu.VMEM((tm, D_FF), BF16)],
        compiler_params=_cparams(("arbitrary",)),
        name="swiglu_halfstep",
    )(*xs, *attn, modt, g.reshape(1, D_MODEL), wg, wu, wd, final_norm.reshape(1, D_MODEL))


MLA_TM = 512


def _swap_halves(x):
    lane = lax.broadcasted_iota(jnp.int32, x.shape, 1)
    fwd = pltpu.roll(x, ROPE_PAD - AXIS_FREQS, axis=1)
    bwd = pltpu.roll(x, AXIS_FREQS, axis=1)
    return jnp.where(lane % (2 * AXIS_FREQS) < AXIS_FREQS, fwd, bwd)


def _rope(x, cos, sin):
    return x * cos + _swap_halves(x) * sin


MLA_SUB = SEQ


def _kv_up(ckv, kpe_pad, wuk_ref, wuvt_ref, k_ref, vt_ref, rows):
    c = ckv.astype(BF16)
    kn = jnp.dot(c, wuk_ref[...], preferred_element_type=F32).astype(BF16)
    vt = lax.dot_general(wuvt_ref[...], c, _NT, preferred_element_type=F32)
    vt_ref[:, rows] = vt.astype(BF16)
    kp = kpe_pad.astype(BF16)
    for hd in range(N_HEADS):
        k_ref[rows, hd * HEAD_PAD:hd * HEAD_PAD + QK_NOPE] = kn[:, hd * QK_NOPE:(hd + 1) * QK_NOPE]
        k_ref[rows, hd * HEAD_PAD + QK_NOPE:(hd + 1) * HEAD_PAD] = kp
    return vt


def _mla_proj_kernel(x_ref, mod_ref, g_ref, wdq_ref, qn_ref, wuq_ref, wdkv_ref, kvn_ref, wuk_ref,
                     wuvt_ref, cos_ref, sin_ref, q_ref, k_ref, vt_ref, v_ref, ckv_ref, kpe_ref):
    states = []
    for r in range(MLA_TM // MLA_SUB):
        rows = slice(r * MLA_SUB, (r + 1) * MLA_SUB)
        h = _mod_rms(x_ref[rows, :], g_ref[...], mod_ref, 3).astype(BF16)
        cq = _rms(jnp.dot(h, wdq_ref[...], preferred_element_type=F32), qn_ref[...]).astype(BF16)
        q = jnp.dot(cq, wuq_ref[...], preferred_element_type=F32)
        cos = cos_ref[rows, :]
        sin = sin_ref[rows, :]
        for hd in range(N_HEADS):
            lo = hd * HEAD_PAD
            q_ref[rows, lo:lo + QK_NOPE] = (q[:, lo:lo + QK_NOPE] * Q_SCALE).astype(BF16)
            qr = _rope(q[:, lo + QK_NOPE:lo + HEAD_PAD], cos, sin)
            q_ref[rows, lo + QK_NOPE:lo + HEAD_PAD] = (qr * Q_SCALE).astype(BF16)
        kv = jnp.dot(h, wdkv_ref[...], preferred_element_type=F32)
        ckv = _rms(kv[:, :KV_LORA], kvn_ref[...])
        kpe = _rope(kv[:, KV_LORA:], cos, sin)
        vt = _kv_up(ckv, kpe, wuk_ref, wuvt_ref, k_ref, vt_ref, rows)
        states.append((rows, ckv, kpe[:, :QK_ROPE], vt))

    @pl.when(pl.program_id(0) < N_CTX // MLA_TM)
    def _():
        for r, (rows, ckv, kpe, vt) in enumerate(states):
            ckv_ref[r] = ckv
            kpe_ref[r] = kpe
            v_ref[rows, :] = vt.T.astype(BF16)


def _mla_proj(x, modt, l, g, wdq, qn, wuq, wdkv, kvn, wuk, wuvt, cos_t, sin_t):
    tm = MLA_TM
    n_ctx = N_CTX // tm
    per_batch = DEC_SEQ // tm

    def rope_idx(i):
        return jnp.where(i < n_ctx, per_batch, (i - n_ctx) % per_batch)

    full = lambda shape: pl.BlockSpec(shape, lambda i: (0, 0))
    rows = lambda w: pl.BlockSpec((tm, w), lambda i: (i, 0))
    state = lambda w: pl.BlockSpec((tm // SEQ, SEQ, w),
                                   lambda i: (jnp.minimum(i, n_ctx - 1), 0, 0))
    return pl.pallas_call(
        _mla_proj_kernel,
        out_shape=(
            jax.ShapeDtypeStruct((N_TOK, N_HEADS * HEAD_PAD), BF16),
            jax.ShapeDtypeStruct((N_TOK, N_HEADS * HEAD_PAD), BF16),
            jax.ShapeDtypeStruct((N_HEADS * V_DIM, N_TOK), BF16),
            jax.ShapeDtypeStruct((N_CTX, N_HEADS * V_DIM), BF16),
            jax.ShapeDtypeStruct((BATCH, SEQ, KV_LORA), F32),
            jax.ShapeDtypeStruct((BATCH, SEQ, QK_ROPE), F32),
        ),
        grid=(N_TOK // tm,),
        in_specs=[
            rows(D_MODEL),
            pl.BlockSpec((None, None, N_MOD, D_MODEL), lambda i: (l, _group(i, tm), 0, 0)),
            full((1, D_MODEL)),
            full((D_MODEL, Q_LORA)),
            full((1, Q_LORA)),
            full((Q_LORA, N_HEADS * HEAD_PAD)),
            full((D_MODEL, KV_LORA + ROPE_PAD)),
            full((1, KV_LORA)),
            full((KV_LORA, N_HEADS * QK_NOPE)),
            full((N_HEADS * V_DIM, KV_LORA)),
            pl.BlockSpec((tm, ROPE_PAD), lambda i: (rope_idx(i), 0)),
            pl.BlockSpec((tm, ROPE_PAD), lambda i: (rope_idx(i), 0)),
        ],
        out_specs=(rows(N_HEADS * HEAD_PAD), rows(N_HEADS * HEAD_PAD),
                   pl.BlockSpec((N_HEADS * V_DIM, tm), lambda i: (0, i)),
                   pl.BlockSpec((tm, N_HEADS * V_DIM), lambda i: (jnp.minimum(i, n_ctx - 1), 0)),
                   state(KV_LORA), state(QK_ROPE)),
        compiler_params=_cparams(("arbitrary",)),
        name="mla_projections",
    )(x, modt, g.reshape(1, D_MODEL), wdq, qn.reshape(1, Q_LORA), wuq, wdkv,
      kvn.reshape(1, KV_LORA), wuk, wuvt, cos_t, sin_t)


def _cache_kv_kernel(ckv_ref, kpe_ref, wuk_ref, wuvt_ref, k_ref, vt_ref):
    _kv_up(ckv_ref[...], kpe_ref[...], wuk_ref, wuvt_ref, k_ref, vt_ref, slice(None))


def _cache_kv(ckv, kpe_pad, wuk, wuvt):
    n = DEC_BATCH * PAST_LEN
    tm = PAST_LEN
    full = lambda shape: pl.BlockSpec(shape, lambda i: (0, 0))
    rows = lambda w: pl.BlockSpec((tm, w), lambda i: (i, 0))
    return pl.pallas_call(
        _cache_kv_kernel,
        out_shape=(jax.ShapeDtypeStruct((n, N_HEADS * HEAD_PAD), BF16),
                   jax.ShapeDtypeStruct((N_HEADS * V_DIM, n), BF16)),
        grid=(n // tm,),
        in_specs=[rows(KV_LORA), rows(ROPE_PAD),
                  full((KV_LORA, N_HEADS * QK_NOPE)), full((N_HEADS * V_DIM, KV_LORA))],
        out_specs=(rows(N_HEADS * HEAD_PAD), pl.BlockSpec((N_HEADS * V_DIM, tm), lambda i: (0, i))),
        compiler_params=_cparams(("parallel",)),
        name="cache_kv_up",
    )(ckv, kpe_pad, wuk, wuvt)


_NT = (((1,), (1,)), ((), ()))


def _ctx_attn_kernel(q_ref, k_ref, v_ref, o_ref):
    for hd in range(N_HEADS):
        q = q_ref[:, hd * HEAD_PAD:(hd + 1) * HEAD_PAD]
        k = k_ref[:, hd * HEAD_PAD:(hd + 1) * HEAD_PAD]
        s = lax.dot_general(q, k, _NT, preferred_element_type=F32)
        p = jnp.exp2(s - jnp.max(s, axis=-1, keepdims=True))
        o = jnp.dot(p.astype(BF16), v_ref[:, hd * V_DIM:(hd + 1) * V_DIM],
                    preferred_element_type=F32)
        o = o / jnp.sum(p, axis=-1, keepdims=True)
        o_ref[:, hd * V_DIM:(hd + 1) * V_DIM] = o.astype(BF16)


def _ctx_attn(q, k, v):
    rows = lambda w: pl.BlockSpec((SEQ, w), lambda b: (b, 0))
    return pl.pallas_call(
        _ctx_attn_kernel,
        out_shape=jax.ShapeDtypeStruct((N_CTX, N_HEADS * V_DIM), BF16),
        grid=(BATCH,),
        in_specs=[rows(N_HEADS * HEAD_PAD), rows(N_HEADS * HEAD_PAD), rows(N_HEADS * V_DIM)],
        out_specs=rows(N_HEADS * V_DIM),
        compiler_params=_cparams(("parallel",)),
        name="context_attention",
    )(q, k, v)


LAT_TQ = 256
LAT_KC = 512
LAT_KEYS = PAST_LEN + DEC_SEQ
SUBLANES = 8


def _lat_attn_kernel(q_ref, kc_ref, kl_ref, vct_ref, vlt_ref, o_ref, s_ref, m_ref):
    tq = q_ref.shape[0]
    kc = LAT_KC
    n_cache = PAST_LEN // kc
    n_chunks = LAT_KEYS // kc
    step = pl.program_id(2)

    @pl.when((pl.program_id(0) == 0) & (pl.program_id(1) == 0) & (step == 0))
    def _():
        s_ref[1] = jnp.zeros(s_ref.shape[1:], F32)
        m_ref[1] = jnp.zeros(m_ref.shape[1:], F32)

    def rows8(x, op):
        return op(x.reshape(kc // SUBLANES, SUBLANES, tq), axis=0)

    def body(cur):
        prev = 1 - cur
        q = q_ref[...]
        m = jnp.max(m_ref[prev], axis=0, keepdims=True)
        mpart = None
        lpart = jnp.zeros((SUBLANES, tq), F32)
        acc = jnp.zeros((V_DIM, tq), F32)
        for c in range(n_chunks):
            rows = slice(c * kc, (c + 1) * kc)
            if c < n_cache:
                k = kc_ref[rows, :]
                vt = vct_ref[:, rows]
            else:
                lat = slice((c - n_cache) * kc, (c - n_cache + 1) * kc)
                k = kl_ref[lat, :]
                vt = vlt_ref[:, lat]
            st = lax.dot_general(k, q, _NT, preferred_element_type=F32)
            s_ref[cur, rows, :] = st
            cm = rows8(st, jnp.max)
            mpart = cm if mpart is None else jnp.maximum(mpart, cm)
            p = jnp.exp2(s_ref[prev, rows, :] - m)
            lpart = lpart + rows8(p, jnp.sum)
            acc = acc + jnp.dot(vt, p.astype(BF16), preferred_element_type=F32)
        m_ref[cur] = mpart
        denom = jnp.sum(lpart, axis=0, keepdims=True)
        o_ref[...] = (acc / denom).T.astype(BF16)

    for parity in range(2):
        pl.when(step % 2 == parity)(functools.partial(body, parity))


def _lat_attn(q, k, vt, kc, vct):
    tq = LAT_TQ
    q_off = N_CTX // tq
    l_off = N_CTX // DEC_SEQ
    nq = DEC_SEQ // tq
    return pl.pallas_call(
        _lat_attn_kernel,
        out_shape=jax.ShapeDtypeStruct((N_LAT, N_HEADS * V_DIM), BF16),
        grid=(DEC_BATCH, N_HEADS, nq + 1),
        in_specs=[
            pl.BlockSpec((tq, HEAD_PAD),
                         lambda b, h, i: (q_off + b * nq + jnp.minimum(i, nq - 1), h)),
            pl.BlockSpec((PAST_LEN, HEAD_PAD), lambda b, h, i: (b, h)),
            pl.BlockSpec((DEC_SEQ, HEAD_PAD), lambda b, h, i: (l_off + b, h)),
            pl.BlockSpec((V_DIM, PAST_LEN), lambda b, h, i: (h, b)),
            pl.BlockSpec((V_DIM, DEC_SEQ), lambda b, h, i: (h, l_off + b)),
        ],
        out_specs=pl.BlockSpec((tq, V_DIM),
                               lambda b, h, i: (b * nq + jnp.maximum(i - 1, 0), h)),
        scratch_shapes=[pltpu.VMEM((2, LAT_KEYS, tq), F32), pltpu.VMEM((2, SUBLANES, tq), F32)],
        compiler_params=_cparams(("arbitrary", "arbitrary", "arbitrary")),
        name="latent_attention",
    )(q, kc, k, vct, vt)


FCH_TM = 1024


def _fourier_ch_kernel(x_ref, mod_ref, g_ref, cs_ref, a_ref, b_ref):
    h = _mod_rms(x_ref[...], g_ref[...], mod_ref, 3).astype(BF16)
    for gi in range(N_FGROUPS):
        lo = gi * FGROUP_DIM
        r = jnp.dot(h[:, lo:lo + FGROUP_DIM], cs_ref[...], preferred_element_type=F32)
        a_ref[:, lo:lo + FGROUP_DIM] = r[:, :FGROUP_DIM].astype(BF16)
        b_ref[:, lo:lo + FGROUP_DIM] = r[:, FGROUP_DIM:].astype(BF16)


def _fourier_ch(x, modt, l, g, ch_tab):
    tm = FCH_TM
    rows = pl.BlockSpec((tm, D_MODEL), lambda i: (i, 0))
    return pl.pallas_call(
        _fourier_ch_kernel,
        out_shape=(jax.ShapeDtypeStruct((N_TOK, D_MODEL), BF16),
                   jax.ShapeDtypeStruct((N_TOK, D_MODEL), BF16)),
        grid=(N_TOK // tm,),
        in_specs=[
            rows,
            pl.BlockSpec((None, None, N_MOD, D_MODEL), lambda i: (l, _group(i, tm), 0, 0)),
            pl.BlockSpec((1, D_MODEL), lambda i: (0, 0)),
            pl.BlockSpec((FGROUP_DIM, 2 * FGROUP_DIM), lambda i: (0, 0)),
        ],
        out_specs=(rows, rows),
        compiler_params=_cparams(("parallel",)),
        name="fourier_channel_dft",
    )(x, modt, g.reshape(1, D_MODEL), ch_tab)


def _fourier_epilogue(f, x_ref, w_ref, bias_ref, mod_ref):
    mixed = jnp.dot(f.astype(BF16), w_ref[...], preferred_element_type=F32) + bias_ref[...]
    return x_ref[...] + mod_ref[5:6, :] * mixed


def _fourier_ctx_kernel(a_ref, b_ref, c_ref, s_ref, x_ref, w_ref, bias_ref, mod_ref, y_ref):
    f = jnp.dot(c_ref[...], a_ref[...], preferred_element_type=F32)
    f = f + jnp.dot(s_ref[...], b_ref[...], preferred_element_type=F32)
    y_ref[...] = _fourier_epilogue(f, x_ref, w_ref, bias_ref, mod_ref)


def _fourier_ctx(a, b, cos_t, nsin_t, x, w, bias, modt, l):
    rows = lambda dt: pl.BlockSpec((SEQ, D_MODEL), lambda i: (i, 0))
    full = lambda shape: pl.BlockSpec(shape, lambda i: (0, 0))
    return pl.pallas_call(
        _fourier_ctx_kernel,
        out_shape=jax.ShapeDtypeStruct((N_CTX, D_MODEL), F32),
        grid=(BATCH,),
        in_specs=[rows(BF16), rows(BF16), full((SEQ, SEQ)), full((SEQ, SEQ)), rows(F32),
                  full((D_MODEL, D_MODEL)), full((1, D_MODEL)),
                  pl.BlockSpec((None, None, N_MOD, D_MODEL), lambda i: (l, 0, 0, 0))],
        out_specs=rows(F32),
        compiler_params=_cparams(("parallel",)),
        name="fourier_context_positions",
    )(a, b, cos_t, nsin_t, x, w, bias.reshape(1, D_MODEL), modt)


FLAT_TM = 512


def _fourier_lat(a, b, cos_t, nsin_t, x, w, bias, modt, l):
    tm = FLAT_TM
    ni = DEC_SEQ // tm
    r_off = N_CTX // tm
    b_off = N_CTX // DEC_SEQ
    src = pl.BlockSpec((DEC_SEQ, D_MODEL), lambda bb, i: (b_off + bb, 0),
                       pipeline_mode=pl.Buffered(1))
    tab = pl.BlockSpec((tm, DEC_SEQ), lambda bb, i: (i, 0))
    return pl.pallas_call(
        _fourier_ctx_kernel,
        out_shape=jax.ShapeDtypeStruct((N_LAT, D_MODEL), F32),
        grid=(DEC_BATCH, ni),
        in_specs=[src, src, tab, tab,
                  pl.BlockSpec((tm, D_MODEL), lambda bb, i: (r_off + bb * ni + i, 0)),
                  pl.BlockSpec((D_MODEL, D_MODEL), lambda bb, i: (0, 0)),
                  pl.BlockSpec((1, D_MODEL), lambda bb, i: (0, 0)),
                  pl.BlockSpec((None, None, N_MOD, D_MODEL), lambda bb, i: (l, 1 + bb, 0, 0))],
        out_specs=pl.BlockSpec((tm, D_MODEL), lambda bb, i: (bb * ni + i, 0)),
        compiler_params=_cparams(("arbitrary", "arbitrary")),
        name="fourier_latent_positions",
    )(a, b, cos_t, nsin_t, x, w, bias.reshape(1, D_MODEL), modt)


TAB_R = 64
TAB_TM = 256


def _dft_table_kernel(ar_ref, ai_ref, br_ref, bi_ref, cos_ref, nsin_ref):
    br = br_ref[...]
    bi = bi_ref[...]
    for r in range(TAB_TM // TAB_R):
        ar = ar_ref[r]
        ai = ai_ref[r]
        rows = slice(r * TAB_R, (r + 1) * TAB_R)
        cos_ref[rows, :] = (ar * br - ai * bi).astype(BF16)
        nsin_ref[rows, :] = (ai * br + ar * bi).astype(BF16)


def _latent_dft_tables():
    n = DEC_SEQ
    assert n == TAB_R * TAB_R
    idx = jnp.arange(TAB_R, dtype=jnp.int32)
    prod = idx[:, None] * idx[None, :]
    ang_c = (prod % TAB_R).astype(F32) * (2.0 * math.pi / TAB_R)
    ang_f = prod.astype(F32) * (2.0 * math.pi / n)
    cr, ci = jnp.cos(ang_c), -jnp.sin(ang_c)
    scale = 1.0 / math.sqrt(n)
    fr, fi = jnp.cos(ang_f) * scale, jnp.sin(ang_f) * (-scale)
    ar = jnp.tile(cr, (1, TAB_R)).reshape(TAB_R, 1, n)
    ai = jnp.tile(ci, (1, TAB_R)).reshape(TAB_R, 1, n)
    br = (cr[:, :, None] * fr[:, None, :] - ci[:, :, None] * fi[:, None, :]).reshape(TAB_R, n)
    bi = (cr[:, :, None] * fi[:, None, :] + ci[:, :, None] * fr[:, None, :]).reshape(TAB_R, n)
    per = TAB_TM // TAB_R
    coarse = pl.BlockSpec((per, 1, n), lambda i: (i, 0, 0))
    fine = pl.BlockSpec((TAB_R, n), lambda i: (0, 0))
    out = pl.BlockSpec((TAB_TM, n), lambda i: (i, 0))
    return pl.pallas_call(
        _dft_table_kernel,
        out_shape=(jax.ShapeDtypeStruct((n, n), BF16), jax.ShapeDtypeStruct((n, n), BF16)),
        grid=(n // TAB_TM,),
        in_specs=[coarse, coarse, fine, fine],
        out_specs=(out, out),
        compiler_params=_cparams(("parallel",)),
        name="latent_dft_tables",
    )(ar, ai, br, bi)


def _dft_tables(n, scale):
    idx = jnp.arange(n, dtype=jnp.int32)
    ang = ((idx[:, None] * idx[None, :]) % n).astype(F32) * (2.0 * math.pi / n)
    return (jnp.cos(ang) * scale).astype(BF16), (jnp.sin(ang) * (-scale)).astype(BF16)


def _rope_tables(tm):
    rows = DEC_SEQ // GRID_W
    inv = 1.0 / (ROPE_THETA ** (jnp.arange(AXIS_FREQS, dtype=F32) / AXIS_FREQS))
    ar = jnp.arange(rows, dtype=F32)[:, None] * inv
    ac = jnp.arange(GRID_W, dtype=F32)[:, None] * inv
    cos_r, sin_r = (jnp.repeat(t, GRID_W, axis=0) for t in (jnp.cos(ar), jnp.sin(ar)))
    cos_c, sin_c = (jnp.tile(t, (rows, 1)) for t in (jnp.cos(ac), jnp.sin(ac)))
    pad = jnp.zeros((DEC_SEQ, ROPE_PAD - QK_ROPE), F32)
    cos = jnp.concatenate([cos_r, cos_r, cos_c, cos_c, pad + 1.0], axis=1)
    sin = jnp.concatenate([-sin_r, sin_r, -sin_c, sin_c, pad], axis=1)
    cos = jnp.concatenate([cos, jnp.ones((tm, ROPE_PAD), F32)], axis=0)
    sin = jnp.concatenate([sin, jnp.zeros((tm, ROPE_PAD), F32)], axis=0)
    return cos, sin


def _pad_heads(w, part, width):
    k = w.shape[0]
    w = w.reshape(k, N_HEADS, -1)[:, :, part]
    return jnp.pad(w, ((0, 0), (0, 0), (0, width - w.shape[-1])))


def kernel(x_prompt, x_sample, cache_ckv, cache_kpe, c, c_ctx, w_mod, b_mod, norm_g, ffn_wg, ffn_wu,
           ffn_wd, mla_w_dq, mla_q_norm, mla_w_uq, mla_w_dkv, mla_kv_norm, mla_w_ukv, mla_w_o,
           fourier_w, fourier_b, final_norm):
    xs = (x_prompt.reshape(N_CTX, D_MODEL), x_sample.reshape(N_LAT, D_MODEL))
    cond = jnp.concatenate([c_ctx[None, :], c, jnp.zeros((N_GROUPS - 1 - DEC_BATCH, D_MODEL), F32)], axis=0)
    modt = _modulation(cond, w_mod, b_mod)

    wg = ffn_wg.astype(BF16)
    wu = ffn_wu.astype(BF16)
    wd = ffn_wd.astype(BF16)

    rope_cos, rope_sin = _rope_tables(MLA_TM)
    ch_cos, ch_nsin = _dft_tables(FGROUP_DIM, 1.0 / math.sqrt(FGROUP_DIM))
    ch_tab = jnp.concatenate([ch_cos, -ch_nsin], axis=1)
    ctx_cos, ctx_nsin = _dft_tables(SEQ, 1.0 / math.sqrt(SEQ))
    lat_cos, lat_nsin = _latent_dft_tables()

    ckv_states = []
    kpe_states = []
    for l in range(DEPTH):
        j = l // 2
        x = _ffn(xs if l == 0 else (x,), modt, l, 0, norm_g[l, 0], wg, wu, wd, final_norm, False)
        if l % 2 == 0:
            wuq = mla_w_uq[j]
            wuq_p = jnp.concatenate(
                [_pad_heads(wuq, slice(0, QK_NOPE), QK_NOPE),
                 _pad_heads(wuq, slice(QK_NOPE, QK_NOPE + QK_ROPE), ROPE_PAD)], axis=-1)
            wuq_p = wuq_p.reshape(Q_LORA, N_HEADS * HEAD_PAD).astype(BF16)
            wdkv_p = jnp.pad(mla_w_dkv[j], ((0, 0), (0, ROPE_PAD - QK_ROPE))).astype(BF16)
            wukv = mla_w_ukv[j]
            wuk = _pad_heads(wukv, slice(0, QK_NOPE), QK_NOPE).reshape(KV_LORA, -1).astype(BF16)
            wuvt = _pad_heads(wukv, slice(QK_NOPE, QK_NOPE + V_DIM), V_DIM).reshape(KV_LORA, -1).T.astype(BF16)
            q, k, vt, v_ctx, new_ckv, new_kpe = _mla_proj(
                x, modt, l, norm_g[l, 1], mla_w_dq[j].astype(BF16), mla_q_norm[j], wuq_p, wdkv_p,
                mla_kv_norm[j], wuk, wuvt, rope_cos, rope_sin)
            ckv_states.append(new_ckv)
            kpe_states.append(new_kpe)
            kc, vct = _cache_kv(
                cache_ckv[:, j].reshape(DEC_BATCH * PAST_LEN, KV_LORA),
                jnp.pad(cache_kpe[:, j].reshape(DEC_BATCH * PAST_LEN, QK_ROPE),
                        ((0, 0), (0, ROPE_PAD - QK_ROPE))),
                wuk, wuvt)
            o_ctx = _ctx_attn(q, k, v_ctx)
            o_lat = _lat_attn(q, k, vt, kc, vct)
            mixed = (x,)
            attn = (o_ctx, o_lat, mla_w_o[j].astype(BF16))
        else:
            attn = ()
            a, b = _fourier_ch(x, modt, l, norm_g[l, 1], ch_tab)
            fw = fourier_w[j].astype(BF16)
            mixed = (_fourier_ctx(a, b, ctx_cos, ctx_nsin, x, fw, fourier_b[j], modt, l),
                     _fourier_lat(a, b, lat_cos, lat_nsin, x, fw, fourier_b[j], modt, l))
        x = _ffn(mixed, modt, l, 1, norm_g[l, 2], wg, wu, wd, final_norm, l == DEPTH - 1, attn)

    y_ctx, y_lat = x
    return (y_ctx.reshape(BATCH, SEQ, D_MODEL), y_lat.reshape(DEC_BATCH, DEC_SEQ, D_MODEL),
            jnp.stack(ckv_states, axis=1), jnp.stack(kpe_states, axis=1))
```
